```python
import math
import jax, jax.numpy as jnp
from jax import lax
import numpy as np

D_MODEL = 1024
BATCH = 4
SEQ = 8192
DEPTH = 1

HEAD_DIM = 64
RWKV_HEADS = 8
SB_HEADS = 8
RWKV_WIDTH = RWKV_HEADS * HEAD_DIM
SB_WIDTH = SB_HEADS * HEAD_DIM
D_FF = -(-8 * D_MODEL // (3 * 256)) * 256
DECAY_LORA = max(32, int(round(1.8 * math.sqrt(D_MODEL) / 32)) * 32)
AAA_LORA = max(32, int(round(1.8 * math.sqrt(D_MODEL) / 32)) * 32)
GATE_LORA = max(32, int(round(0.6 * D_MODEL ** 0.8 / 32)) * 32)
IN_COLS = 3 * RWKV_WIDTH + 3 * SB_WIDTH + 2 * D_MODEL
SB_BLOCK = 128
NORM_EPS = 1e-6
LNX_EPS = 64e-5

kernel_name = 'hybrid_rwkv7_stickbreak_block'


def _rmsnorm(x, g):
    xf = x.astype(jnp.float32)
    y = xf * lax.rsqrt(jnp.mean(xf * xf, axis=-1, keepdims=True) + NORM_EPS)
    return (y * g.astype(jnp.float32)).astype(x.dtype)


def _modulate(h, shift, scale):
    return h * (1 + scale[:, None, :]) + shift[:, None, :]


def _shift(t):
    return jnp.pad(t, ((0, 0), (1, 0), (0, 0)))[:, :-1]


def _rwkv7_time_mix(h, r, k, v, mu_rkv, mu_wag, w0, w1, w2, a0, a1, a2, g1, g2,
                    k_k, k_a, r_k, lnx_w, lnx_b):
    B, T, _ = h.shape
    H, N = RWKV_HEADS, HEAD_DIM
    f32 = jnp.float32
    r = r + (_shift(r) - r) * mu_rkv[0]
    k = k + (_shift(k) - k) * mu_rkv[1]
    v = v + (_shift(v) - v) * mu_rkv[2]
    xx = _shift(h) - h
    xw = h + xx * mu_wag[0]
    xa = h + xx * mu_wag[1]
    xg = h + xx * mu_wag[2]
    w_log = -jax.nn.softplus(-(w0 + jnp.tanh(xw @ w1) @ w2)) - 0.5
    a = jax.nn.sigmoid(a0 + (xa @ a1) @ a2)
    g = jax.nn.sigmoid(xg @ g1) @ g2
    heads = lambda t: t.astype(f32).reshape(B, T, H, N)
    kk = heads(k * k_k)
    kk = kk / jnp.maximum(jnp.sqrt(jnp.sum(kk * kk, axis=-1, keepdims=True)), 1e-12)
    k = heads(k * (1 + (a - 1) * k_a))
    r, v, a = heads(r), heads(v), heads(a)
    decay = jnp.exp(-jnp.exp(heads(w_log)))
    removal = -kk
    replacement = kk * a

    def step(S, inp):
        r_t, d_t, k_t, v_t, a_t, b_t = inp
        sa = jnp.einsum('bhvk,bhk->bhv', S, a_t)
        S = S * d_t[:, :, None, :] + sa[..., None] * b_t[:, :, None, :] + v_t[..., None] * k_t[:, :, None, :]
        y_t = jnp.einsum('bhvk,bhk->bhv', S, r_t)
        return S, y_t

    tm = lambda t: jnp.swapaxes(t, 0, 1)
    S0 = jnp.zeros((B, H, N, N), f32)
    _, y = lax.scan(step, S0, (tm(r), tm(decay), tm(k), tm(v), tm(removal), tm(replacement)))
    y = tm(y)
    mean = jnp.mean(y, axis=-1, keepdims=True)
    var = jnp.mean(jnp.square(y - mean), axis=-1, keepdims=True)
    y = (y - mean) * lax.rsqrt(var + LNX_EPS) * lnx_w.astype(f32).reshape(H, N) + lnx_b.astype(f32).reshape(H, N)
    bonus = jnp.sum(r * k * r_k.astype(f32), axis=-1, keepdims=True) * v
    y = (y + bonus).reshape(B, T, H * N) * g.astype(f32)
    return y.astype(h.dtype)


def _stick_breaking(q, k, v):
    B, T, _ = q.shape
    dt = q.dtype
    H, N = SB_HEADS, HEAD_DIM
    to_heads = lambda t: t.astype(jnp.float32).reshape(B, T, H, N).transpose(0, 2, 1, 3)
    q, k, v = to_heads(q) * (N ** -0.5), to_heads(k), to_heads(v)
    outs = []
    for i in range(T // SB_BLOCK):
        t0 = i * SB_BLOCK
        t1 = t0 + SB_BLOCK
        z = jnp.einsum('bhtd,bhsd->bhts', q[:, :, t0:t1], k[:, :, :t1])
        past = jnp.arange(t1)[None, :] < (t0 + jnp.arange(SB_BLOCK))[:, None]
        log_keep = jnp.where(past, jax.nn.log_sigmoid(-z), 0.0)
        log_between = lax.cumsum(log_keep, axis=3, reverse=True) - log_keep
        attn = jnp.where(past, jnp.exp(jax.nn.log_sigmoid(z) + log_between), 0.0)
        outs.append(jnp.einsum('bhts,bhsd->bhtd', attn, v[:, :, :t1]))
    o = jnp.concatenate(outs, axis=2)
    return o.transpose(0, 2, 1, 3).reshape(B, T, H * N).astype(dt)


def setup_inputs(seed: int = 0) -> dict:
    key = jax.random.key(seed)
    ks = iter(jax.random.split(key, 40))
    L, D, W, S = DEPTH, D_MODEL, RWKV_WIDTH, SB_WIDTH
    nrm = lambda shape, scale: jax.random.normal(next(ks), shape, jnp.float32) * scale
    uni = lambda shape, lo, hi: jax.random.uniform(next(ks), shape, jnp.float32, lo, hi)
    return {
        'x': nrm((BATCH, SEQ, D), 1.0),
        'c': nrm((BATCH, D), 1.0),
        'w_ada': nrm((L, D, 6 * D), D ** -0.5),
        'b_ada': nrm((L, 6 * D), 0.02),
        'norm1_g': 1.0 + nrm((L, D), 0.02),
        'w_in': nrm((L, D, IN_COLS), D ** -0.5),
        'mu_rkv': uni((L, 3, W), 0.0, 1.0),
        'mu_wag': uni((L, 3, D), 0.0, 1.0),
        'w0': uni((L, W), -5.0, 0.5),
        'w1': nrm((L, D, DECAY_LORA), D ** -0.5),
        'w2': nrm((L, DECAY_LORA, W), 0.1 * DECAY_LORA ** -0.5),
        'a0': nrm((L, W), 0.5),
        'a1': nrm((L, D, AAA_LORA), D ** -0.5),
        'a2': nrm((L, AAA_LORA, W), AAA_LORA ** -0.5),
        'g1': nrm((L, D, GATE_LORA), D ** -0.5),
        'g2': nrm((L, GATE_LORA, W), GATE_LORA ** -0.5),
        'k_k': 0.85 + nrm((L, W), 0.02),
        'k_a': 1.0 + nrm((L, W), 0.02),
        'r_k': nrm((L, RWKV_HEADS, HEAD_DIM), 0.1),
        'lnx_w': 1.0 + nrm((L, W), 0.02),
        'lnx_b': nrm((L, W), 0.02),
        'w_proj_a': nrm((L, W, D), W ** -0.5),
        'w_proj_b': nrm((L, S, D), S ** -0.5),
        'w_out': nrm((L, D, D), D ** -0.5),
        'norm2_g': 1.0 + nrm((L, D), 0.02),
        'w_ffn_in': nrm((L, D, 2 * D_FF), D ** -0.5),
        'w_ffn_out': nrm((L, D_FF, D), D_FF ** -0.5),
        'final_g': 1.0 + nrm((D,), 0.02),
    }


def reference(x, c, w_ada, b_ada, norm1_g, w_in, mu_rkv, mu_wag, w0, w1, w2, a0, a1, a2,
              g1, g2, k_k, k_a, r_k, lnx_w, lnx_b, w_proj_a, w_proj_b, w_out, norm2_g,
              w_ffn_in, w_ffn_out, final_g):
    W, S, D = RWKV_WIDTH, SB_WIDTH, D_MODEL
    cuts = [W, 2 * W, 3 * W, 3 * W + S, 3 * W + 2 * S, 3 * W + 3 * S, 3 * W + 3 * S + D]
    c_act = jax.nn.silu(c)
    for l in range(DEPTH):
        mod = c_act @ w_ada[l] + b_ada[l]
        sh1, sc1, gt1, sh2, sc2, gt2 = jnp.split(mod, 6, axis=-1)
        h = _modulate(_rmsnorm(x, norm1_g[l]), sh1, sc1)
        p = h @ w_in[l]
        r, k, v, q_s, k_s, v_s, gate_a, gate_b = jnp.split(p, cuts, axis=-1)
        y_a = _rwkv7_time_mix(h, r, k, v, mu_rkv[l], mu_wag[l], w0[l], w1[l], w2[l],
                              a0[l], a1[l], a2[l], g1[l], g2[l], k_k[l], k_a[l], r_k[l],
                              lnx_w[l], lnx_b[l])
        y_b = _stick_breaking(q_s, k_s, v_s)
        merged = jax.nn.sigmoid(gate_a) * (y_a @ w_proj_a[l]) + jax.nn.sigmoid(gate_b) * (y_b @ w_proj_b[l])
        x = x + gt1[:, None, :] * (merged @ w_out[l])
        h = _modulate(_rmsnorm(x, norm2_g[l]), sh2, sc2)
        u_gate, u_up = jnp.split(h @ w_ffn_in[l], 2, axis=-1)
        x = x + gt2[:, None, :] * ((jax.nn.silu(u_gate) * u_up) @ w_ffn_out[l])
    return _rmsnorm(x, final_g)
```

```python
import functools

import jax
import jax.numpy as jnp
from jax import lax
from jax.experimental import pallas as pl
from jax.experimental.pallas import tpu as pltpu

F32 = jnp.float32
BF16 = jnp.bfloat16

HEAD_DIM = 64
LANES = 128
NORM_EPS = 1e-6
LNX_EPS = 64e-5
RWKV_CHUNK = 64
SB_TILE = 128
EXP_UNDERFLOW = 104.0
VMEM_LIMIT = 56 * 1024 * 1024


def _cparams(sem):
    return pltpu.CompilerParams(dimension_semantics=sem, vmem_limit_bytes=VMEM_LIMIT)


_NN = (((1,), (0,)), ((), ()))
_NT = (((1,), (1,)), ((), ()))
_TN = (((0,), (0,)), ((), ()))


def _dg(a, b, dn=_NN):
    return lax.dot_general(a, b, dn, preferred_element_type=F32)


def _split(x, n):
    pieces, rem = [], x
    for i in range(n):
        p = rem.astype(BF16)
        pieces.append(p)
        if i + 1 < n:
            rem = rem - p.astype(F32)
    return pieces


def _mm(a, b, dn=_NN, na=1, nb=1):
    pa = _split(a, na) if a.dtype != BF16 else [a]
    pb = _split(b, nb) if b.dtype != BF16 else [b]
    order = max(len(pa), len(pb))
    acc = None
    for i, x in enumerate(pa):
        for j, y in enumerate(pb):
            if i + j < order:
                t = _dg(x, y, dn)
                acc = t if acc is None else acc + t
    return acc


def _sigmoid(x):
    return 1.0 / (1.0 + jnp.exp(-x))


def _log_sigmoid(x):
    return jnp.minimum(x, 0.0) - jnp.log(1.0 + jnp.exp(-jnp.abs(x)))


def _mod_kernel(c_ref, w_ref, b_ref, o_ref):
    c = c_ref[...]
    c_act = c * _sigmoid(c)
    o_ref[...] = _mm(c_act, w_ref[...], na=3, nb=3) + b_ref[...]


def _mod_call(c, w_ada, b_ada):
    B, D = c.shape
    n = w_ada.shape[1]
    return pl.pallas_call(
        _mod_kernel,
        grid=(n // D,),
        in_specs=[pl.BlockSpec((B, D), lambda j: (0, 0)),
                  pl.BlockSpec((D, D), lambda j: (0, j)),
                  pl.BlockSpec((1, D), lambda j: (0, j))],
        out_specs=pl.BlockSpec((B, D), lambda j: (0, j)),
        out_shape=jax.ShapeDtypeStruct((B, n), F32),
        compiler_params=_cparams(("arbitrary",)),
        name="mod",
    )(c, w_ada, b_ada.reshape(1, n))


def _shift_rows(cur, prev_row):
    rolled = pltpu.roll(cur, 1, 0)
    row = lax.broadcasted_iota(jnp.int32, cur.shape, 0)
    return jnp.where(row == 0, prev_row, rolled)


def _inproj_kernel(x_ref, xp_ref, mod_ref, g_ref, wrkv_ref, wqkv_ref, wgate_ref, murkv_ref, muwag_ref,
                   w1_ref, w2_ref, w0_ref, a1_ref, a2_ref, a0_ref, g1_ref, g2_ref,
                   rkv_ref, ld_ref, ai_ref, go_ref, qkv_ref, gate_ref, *, tiles_per_seq):
    first = (pl.program_id(0) % tiles_per_seq) == 0
    shift = mod_ref[0, 0:1, :]
    scale = mod_ref[0, 1:2, :]
    gain = g_ref[...]

    def norm_mod(xv):
        ms = jnp.mean(xv * xv, axis=-1, keepdims=True)
        return (xv * lax.rsqrt(ms + NORM_EPS) * gain) * (1.0 + scale) + shift

    h = norm_mod(x_ref[...])
    hp = jnp.where(first, 0.0, norm_mod(xp_ref[...]))
    hb = h.astype(BF16)

    p = _dg(hb, wrkv_ref[...])
    pp = _dg(hp.astype(BF16), wrkv_ref[...])[7:8, :]
    rkv_ref[...] = p + (_shift_rows(p, pp) - p) * murkv_ref[...]

    qkv_ref[...] = _dg(hb, wqkv_ref[...]).astype(BF16)
    gate_ref[...] = _sigmoid(_dg(hb, wgate_ref[...])).astype(BF16)

    xx = _shift_rows(h, hp[7:8, :]) - h
    xw = (h + xx * muwag_ref[0:1, :]).astype(BF16)
    xa = (h + xx * muwag_ref[1:2, :]).astype(BF16)
    xg = (h + xx * muwag_ref[2:3, :]).astype(BF16)
    u = w0_ref[...] + _dg(jnp.tanh(_dg(xw, w1_ref[...])).astype(BF16), w2_ref[...])
    w_log = _log_sigmoid(u) - 0.5
    ld_ref[...] = -jnp.exp(w_log)
    ai_ref[...] = _sigmoid(a0_ref[...] + _dg(_dg(xa, a1_ref[...]).astype(BF16), a2_ref[...]))
    go_ref[...] = _dg(_sigmoid(_dg(xg, g1_ref[...])).astype(BF16), g2_ref[...])


def _inproj_call(x2, mod3, norm_g, w_rkv, w_qkv, w_gate, mu_rkv, mu_wag, w1, w2, w0, a1, a2, a0, g1, g2,
                 *, seq, tm):
    M, D = x2.shape
    W3 = w_rkv.shape[1]
    W = W3 // 3
    tiles_per_seq = seq // tm
    const = lambda shape: pl.BlockSpec(shape, lambda i: (0,) * len(shape))
    rows = lambda n: pl.BlockSpec((tm, n), lambda i: (i, 0))
    return pl.pallas_call(
        functools.partial(_inproj_kernel, tiles_per_seq=tiles_per_seq),
        grid=(M // tm,),
        in_specs=[rows(D),
                  pl.BlockSpec((8, D), lambda i: (jnp.maximum(i * (tm // 8) - 1, 0), 0)),
                  pl.BlockSpec((1, mod3.shape[1], D), lambda i: (i // tiles_per_seq, 0, 0)),
                  const((1, D)), const(w_rkv.shape), const(w_qkv.shape), const(w_gate.shape),
                  const((1, W3)), const(mu_wag.shape),
                  const(w1.shape), const(w2.shape), const((1, W)),
                  const(a1.shape), const(a2.shape), const((1, W)),
                  const(g1.shape), const(g2.shape)],
        out_specs=[rows(W3), rows(W), rows(W), rows(W), rows(w_qkv.shape[1]), rows(w_gate.shape[1])],
        out_shape=[jax.ShapeDtypeStruct((M, W3), F32),
                   jax.ShapeDtypeStruct((M, W), F32),
                   jax.ShapeDtypeStruct((M, W), F32),
                   jax.ShapeDtypeStruct((M, W), F32),
                   jax.ShapeDtypeStruct((M, w_qkv.shape[1]), BF16),
                   jax.ShapeDtypeStruct((M, w_gate.shape[1]), BF16)],
        compiler_params=_cparams(("arbitrary",)),
        name="inproj",
    )(x2, x2, mod3, norm_g.reshape(1, D), w_rkv, w_qkv, w_gate, mu_rkv.reshape(1, W3), mu_wag,
      w1, w2, w0.reshape(1, W), a1, a2, a0.reshape(1, W), g1, g2)


def _rwkv_kernel(r_ref, k_ref, v_ref, ld_ref, ai_ref, g_ref, kk_ref, ka_ref, rk_ref, lw_ref, lb_ref,
                 y_ref, state_ref, *, chunk, prec):
    C = chunk
    tb = r_ref.shape[0]
    n_chunks = tb // C
    S2 = 2 * C
    na = nb = prec

    @pl.when(pl.program_id(2) == 0)
    def _():
        state_ref[...] = jnp.zeros_like(state_ref)

    lane = lax.broadcasted_iota(jnp.int32, (1, LANES), 1)
    head0 = lane < HEAD_DIM
    li = lax.broadcasted_iota(jnp.int32, (LANES, LANES), 0)
    lj = lax.broadcasted_iota(jnp.int32, (LANES, LANES), 1)
    seg_ones = ((li // HEAD_DIM) == (lj // HEAD_DIM)).astype(BF16)

    def seg_sum(x):
        return _mm(x, seg_ones, na=3)

    r = r_ref[...]
    k = k_ref[...]
    v = v_ref[...]
    ld = ld_ref[...]
    ai = ai_ref[...]

    kk = k * kk_ref[...]
    kk = kk / jnp.maximum(jnp.sqrt(seg_sum(kk * kk)), 1e-12)
    k2 = k * (1.0 + (ai - 1.0) * ka_ref[...])

    ti = lax.broadcasted_iota(jnp.int32, (tb, tb), 0)
    tj = lax.broadcasted_iota(jnp.int32, (tb, tb), 1)
    cum_ones = ((ti // C == tj // C) & (tj <= ti)).astype(BF16)
    cum = _mm(cum_ones, ld, nb=3)

    e_in = jnp.exp(cum)
    a_t = -kk * jnp.exp(cum - ld)
    r_t = r * e_in
    e_neg = jnp.exp(-cum)
    b_t = kk * ai * e_neg
    k_t = k2 * e_neg

    si = lax.broadcasted_iota(jnp.int32, (S2, S2), 0)
    sj = lax.broadcasted_iota(jnp.int32, (S2, S2), 1)
    same = (si // C) == (sj // C)
    strict = same & (sj < si)
    incl = same & (sj <= si)
    eye = si == sj

    def stack(x):
        return jnp.concatenate([jnp.where(head0, x, 0.0), jnp.where(head0, 0.0, x)], axis=0)

    state = state_ref[...]
    ys = []
    for c in range(n_chunks):
        sl = slice(c * C, (c + 1) * C)
        a_s, r_s, b_s, k_s, v_s = stack(a_t[sl]), stack(r_t[sl]), stack(b_t[sl]), stack(k_t[sl]), stack(v[sl])
        sc = _mm(jnp.concatenate([a_s, r_s], axis=0), jnp.concatenate([b_s, k_s], axis=0), _NT, na, nb)
        a_ab = jnp.where(strict, sc[:S2, :S2], 0.0)
        a_ak = jnp.where(strict, sc[:S2, S2:], 0.0)
        a_rb = jnp.where(incl, sc[S2:, :S2], 0.0)
        a_rk = jnp.where(incl, sc[S2:, S2:], 0.0)

        pinv = jnp.where(eye, 1.0, a_ab)
        apow = _mm(a_ab, a_ab, _NN, na, nb)
        n_sq = max(C.bit_length() - 2, 0)
        for _ in range(n_sq - 1):
            both = _mm(jnp.concatenate([pinv, apow], axis=0), apow, _NN, na, nb)
            pinv = pinv + both[:S2]
            apow = both[S2:]
        pinv = pinv + _mm(pinv, apow, _NN, na, nb)

        av = _mm(a_ak, v_s, _NN, na, nb)
        ta = _mm(pinv, jnp.concatenate([a_s, av], axis=1), _NN, na, nb)
        rb = _mm(a_rb, ta, _NN, na, nb)
        r_h = r_s + rb[:, :LANES]
        y_h = rb[:, LANES:] + _mm(a_rk, v_s, _NN, na, nb)

        p_end = e_in[(c + 1) * C - 1:(c + 1) * C, :]
        bt = _mm(b_s * p_end, ta, _TN, na, nb)
        m_mat = jnp.where(eye, p_end, 0.0) + bt[:, :LANES]
        g_mat = bt[:, LANES:] + _mm(k_s * p_end, v_s, _TN, na, nb)

        y_st = _mm(r_h, state, _NN, na, nb) + y_h
        ys.append(y_st[:C] + y_st[C:])
        state = _mm(m_mat, state, _NN, na, nb) + g_mat
    state_ref[...] = state
    y = jnp.concatenate(ys, axis=0) if n_chunks > 1 else ys[0]

    inv_n = 1.0 / HEAD_DIM
    mean = seg_sum(y) * inv_n
    yc = y - mean
    var = seg_sum(yc * yc) * inv_n
    yn = yc * lax.rsqrt(var + LNX_EPS) * lw_ref[...] + lb_ref[...]
    bonus = seg_sum(r * k2 * rk_ref[...]) * v
    y_ref[...] = (yn + bonus) * g_ref[...]


def _rwkv_call(rkv, ld, ai, g, k_k, k_a, r_k, lnx_w, lnx_b, *, batch, seq, tb, prec):
    M, W = ld.shape
    n_slab = W // LANES
    steps = seq // tb
    tok = lambda off: pl.BlockSpec((tb, LANES), lambda b, p, s: (b * steps + s, off + p))
    par = pl.BlockSpec((1, LANES), lambda b, p, s: (0, p))
    return pl.pallas_call(
        functools.partial(_rwkv_kernel, chunk=RWKV_CHUNK, prec=prec),
        grid=(batch, n_slab, steps),
        in_specs=[tok(0), tok(n_slab), tok(2 * n_slab), tok(0), tok(0), tok(0), par, par, par, par, par],
        out_specs=tok(0),
        out_shape=jax.ShapeDtypeStruct((M, W), F32),
        scratch_shapes=[pltpu.VMEM((LANES, LANES), F32)],
        compiler_params=_cparams(("arbitrary", "arbitrary", "arbitrary")),
        name="rwkv",
    )(rkv, rkv, rkv, ld, ai, g, k_k.reshape(1, W), k_a.reshape(1, W), r_k.reshape(1, W),
      lnx_w.reshape(1, W), lnx_b.reshape(1, W))


def _stick_kernel(q_ref, k_ref, v_ref, o_ref, acc_ref, run_ref, *, tile):
    i = pl.program_id(2)
    lane = lax.broadcasted_iota(jnp.int32, (1, LANES), 1)
    head0 = lane < HEAD_DIM
    ti = lax.broadcasted_iota(jnp.int32, (tile, tile), 0)
    tj = lax.broadcasted_iota(jnp.int32, (tile, tile), 1)
    past = tj < ti
    ui = lax.broadcasted_iota(jnp.int32, (tile, tile + LANES), 0)
    uj = lax.broadcasted_iota(jnp.int32, (tile, tile + LANES), 1)
    rev_ones = ((ui > uj) | (uj >= tile)).astype(BF16)

    q = q_ref[...] * jnp.asarray(HEAD_DIM ** -0.5, BF16)
    q_heads = (jnp.where(head0, q, jnp.zeros_like(q)), jnp.where(head0, jnp.zeros_like(q), q))

    def visit(kb, vb, diagonal):
        worst = None
        for h in range(2):
            z = _dg(q_heads[h], kb, _NT)
            ls = _log_sigmoid(z)
            lk = ls - z
            if diagonal:
                lk = jnp.where(past, lk, 0.0)
                run = jnp.zeros((tile, LANES), F32)
            else:
                run = run_ref[h]
            cs = _mm(lk, rev_ones, nb=1, na=3)
            w = jnp.exp(ls + cs[:, :tile] + run)
            if diagonal:
                w = jnp.where(past, w, 0.0)
            pv = _dg(w.astype(BF16), vb)
            run = run + cs[:, tile:]
            run_ref[h] = run
            acc_ref[h] = pv if diagonal else acc_ref[h] + pv
            m = jnp.max(run)
            worst = m if worst is None else jnp.maximum(worst, m)
        return worst

    d0 = pl.multiple_of(i * tile, tile)
    worst = visit(k_ref[pl.ds(d0, tile), :], v_ref[pl.ds(d0, tile), :], True)

    def cond(carry):
        j, worst = carry
        return (j >= 0) & (worst > -EXP_UNDERFLOW)

    def body(carry):
        j, _ = carry
        s0 = pl.multiple_of(j * tile, tile)
        return j - 1, visit(k_ref[pl.ds(s0, tile), :], v_ref[pl.ds(s0, tile), :], False)

    lax.while_loop(cond, body, (i - 1, worst))
    o_ref[...] = jnp.where(head0, acc_ref[0], acc_ref[1]).astype(o_ref.dtype)


def _stick_call(qkv, *, batch, seq):
    M, W3 = qkv.shape
    W = W3 // 3
    n_slab = W // LANES
    tile = SB_TILE
    steps = seq // tile
    return pl.pallas_call(
        functools.partial(_stick_kernel, tile=tile),
        grid=(batch, n_slab, steps),
        in_specs=[pl.BlockSpec((tile, LANES), lambda b, p, i: (b * steps + i, p)),
                  pl.BlockSpec((seq, LANES), lambda b, p, i: (b, n_slab + p)),
                  pl.BlockSpec((seq, LANES), lambda b, p, i: (b, 2 * n_slab + p))],
        out_specs=pl.BlockSpec((tile, LANES), lambda b, p, i: (b * steps + i, p)),
        out_shape=jax.ShapeDtypeStruct((M, W), BF16),
        scratch_shapes=[pltpu.VMEM((2, tile, LANES), F32), pltpu.VMEM((2, tile, LANES), F32)],
        compiler_params=_cparams(("arbitrary", "arbitrary", "arbitrary")),
        name="stick",
    )(qkv, qkv, qkv)


def _merge_kernel(x_ref, ya_ref, yb_ref, gate_ref, mod_ref, wa_ref, wb_ref, wo_ref, o_ref):
    D = x_ref.shape[1]
    pa = _dg(ya_ref[...].astype(BF16), wa_ref[...])
    pb = _dg(yb_ref[...], wb_ref[...])
    merged = gate_ref[:, :D].astype(F32) * pa + gate_ref[:, D:].astype(F32) * pb
    o_ref[...] = x_ref[...] + mod_ref[0, 2:3, :] * _dg(merged.astype(BF16), wo_ref[...])


def _merge_call(x2, ya, yb, gates, mod3, w_pa, w_pb, w_out, *, seq, tm):
    M, D = x2.shape
    tiles_per_seq = seq // tm
    const = lambda shape: pl.BlockSpec(shape, lambda i: (0,) * len(shape))
    rows = lambda n: pl.BlockSpec((tm, n), lambda i: (i, 0))
    return pl.pallas_call(
        _merge_kernel,
        grid=(M // tm,),
        in_specs=[rows(D), rows(ya.shape[1]), rows(yb.shape[1]), rows(gates.shape[1]),
                  pl.BlockSpec((1, mod3.shape[1], D), lambda i: (i // tiles_per_seq, 0, 0)),
                  const(w_pa.shape), const(w_pb.shape), const(w_out.shape)],
        out_specs=rows(D),
        out_shape=jax.ShapeDtypeStruct((M, D), F32),
        compiler_params=_cparams(("arbitrary",)),
        name="merge",
    )(x2, ya, yb, gates, mod3, w_pa, w_pb, w_out)


def _ffn_kernel(x_ref, mod_ref, g2_ref, gf_ref, wg_ref, wu_ref, wd_ref, o_ref, h_ref, acc_ref, *, final_norm):
    j = pl.program_id(1)

    @pl.when(j == 0)
    def _():
        xv = x_ref[...]
        ms = jnp.mean(xv * xv, axis=-1, keepdims=True)
        y = xv * lax.rsqrt(ms + NORM_EPS) * g2_ref[...]
        h_ref[...] = (y * (1.0 + mod_ref[0, 4:5, :]) + mod_ref[0, 3:4, :]).astype(BF16)
        acc_ref[...] = jnp.zeros_like(acc_ref)

    hb = h_ref[...]
    ug = _dg(hb, wg_ref[...])
    uu = _dg(hb, wu_ref[...])
    act = (ug * _sigmoid(ug) * uu).astype(BF16)
    acc_ref[...] += _dg(act, wd_ref[...])

    @pl.when(j == pl.num_programs(1) - 1)
    def _():
        x2 = x_ref[...] + mod_ref[0, 5:6, :] * acc_ref[...]
        if final_norm:
            ms = jnp.mean(x2 * x2, axis=-1, keepdims=True)
            x2 = x2 * lax.rsqrt(ms + NORM_EPS) * gf_ref[...]
        o_ref[...] = x2


def _ffn_call(x1, mod3, norm2_g, final_g, w_g, w_u, w_d, *, seq, tm, tf, final_norm):
    M, D = x1.shape
    F = w_g.shape[1]
    tiles_per_seq = seq // tm
    return pl.pallas_call(
        functools.partial(_ffn_kernel, final_norm=final_norm),
        grid=(M // tm, F // tf),
        in_specs=[pl.BlockSpec((tm, D), lambda i, j: (i, 0)),
                  pl.BlockSpec((1, mod3.shape[1], D), lambda i, j: (i // tiles_per_seq, 0, 0)),
                  pl.BlockSpec((1, D), lambda i, j: (0, 0)),
                  pl.BlockSpec((1, D), lambda i, j: (0, 0)),
                  pl.BlockSpec((D, tf), lambda i, j: (0, j)),
                  pl.BlockSpec((D, tf), lambda i, j: (0, j)),
                  pl.BlockSpec((tf, D), lambda i, j: (j, 0))],
        out_specs=pl.BlockSpec((tm, D), lambda i, j: (i, 0)),
        out_shape=jax.ShapeDtypeStruct((M, D), F32),
        scratch_shapes=[pltpu.VMEM((tm, D), BF16), pltpu.VMEM((tm, D), F32)],
        compiler_params=_cparams(("arbitrary", "arbitrary")),
        name="ffn",
    )(x1, mod3, norm2_g.reshape(1, D), final_g.reshape(1, D), w_g, w_u, w_d)


def _tiles(seq):
    pick = lambda want: max(t for t in (8, 16, 32, 64, 128, 256, 512, 1024) if t <= want and seq % t == 0)
    return dict(inproj=pick(256), rwkv=pick(256), merge=pick(512), ffn=pick(512))


def kernel(x, c, w_ada, b_ada, norm1_g, w_in, mu_rkv, mu_wag, w0, w1, w2, a0, a1, a2, g1, g2, k_k, k_a, r_k,
           lnx_w, lnx_b, w_proj_a, w_proj_b, w_out, norm2_g, w_ffn_in, w_ffn_out, final_g):
    B, T, D = x.shape
    depth = w_ada.shape[0]
    W = w0.shape[1]
    S = w_proj_b.shape[1]
    F = w_ffn_out.shape[1]
    assert T % SB_TILE == 0 and W % LANES == 0 and S % LANES == 0
    tiles = _tiles(T)
    tf = F // 2 if (F // 2) % LANES == 0 else F
    x2 = x.reshape(B * T, D)
    for l in range(depth):
        mod3 = _mod_call(c, w_ada[l], b_ada[l]).reshape(B, 6, D)
        wl = w_in[l].astype(BF16)
        rkv, ld, ai, g, qkv, gates = _inproj_call(
            x2, mod3, norm1_g[l], wl[:, :3 * W], wl[:, 3 * W:3 * W + 3 * S], wl[:, 3 * W + 3 * S:],
            mu_rkv[l], mu_wag[l], w1[l].astype(BF16), w2[l].astype(BF16), w0[l],
            a1[l].astype(BF16), a2[l].astype(BF16), a0[l], g1[l].astype(BF16), g2[l].astype(BF16),
            seq=T, tm=tiles["inproj"])
        ya = _rwkv_call(rkv, ld, ai, g, k_k[l], k_a[l], r_k[l], lnx_w[l], lnx_b[l],
                        batch=B, seq=T, tb=tiles["rwkv"], prec=3)
        yb = _stick_call(qkv, batch=B, seq=T)
        x1 = _merge_call(x2, ya, yb, gates, mod3, w_proj_a[l].astype(BF16), w_proj_b[l].astype(BF16),
                         w_out[l].astype(BF16), seq=T, tm=tiles["merge"])
        wf = w_ffn_in[l].astype(BF16)
        x2 = _ffn_call(x1, mod3, norm2_g[l], final_g, wf[:, :F], wf[:, F:], w_ffn_out[l].astype(BF16),
                       seq=T, tm=tiles["ffn"], tf=tf, final_norm=(l == depth - 1))
    return x2.reshape(B, T, D)
```

```python
import functools

import jax
import jax.numpy as jnp
from jax import lax
from jax.experimental import pallas as pl
from jax.experimental.pallas import tpu as pltpu

F32 = jnp.float32
BF16 = jnp.bfloat16

HEAD_DIM = 64
LANES = 128
NORM_EPS = 1e-6
LNX_EPS = 64e-5
RWKV_CHUNK = 64
SB_TILE = 128
EXP_UNDERFLOW = 104.0
VMEM_LIMIT = 56 * 1024 * 1024


def _cparams(sem):
    return pltpu.CompilerParams(dimension_semantics=sem, vmem_limit_bytes=VMEM_LIMIT)


_NN = (((1,), (0,)), ((), ()))
_NT = (((1,), (1,)), ((), ()))
_TN = (((0,), (0,)), ((), ()))


def _dg(a, b, dn=_NN):
    return lax.dot_general(a, b, dn, preferred_element_type=F32)


def _split(x, n):
    pieces, rem = [], x
    for i in range(n):
        p = rem.astype(BF16)
        pieces.append(p)
        if i + 1 < n:
            rem = rem - p.astype(F32)
    return pieces


def _mm(a, b, dn=_NN, na=1, nb=1):
    pa = _split(a, na) if a.dtype != BF16 else [a]
    pb = _split(b, nb) if b.dtype != BF16 else [b]
    order = max(len(pa), len(pb))
    acc = None
    for i, x in enumerate(pa):
        for j, y in enumerate(pb):
            if i + j < order:
                t = _dg(x, y, dn)
                acc = t if acc is None else acc + t
    return acc


def _sigmoid(x):
    return 1.0 / (1.0 + jnp.exp(-x))


def _log_sigmoid(x):
    return jnp.minimum(x, 0.0) - jnp.log(1.0 + jnp.exp(-jnp.abs(x)))


def _mod_kernel(c_ref, w_ref, b_ref, o_ref):
    c = c_ref[...]
    c_act = c * _sigmoid(c)
    o_ref[...] = _mm(c_act, w_ref[...], na=3, nb=3) + b_ref[...]


def _mod_call(c, w_ada, b_ada):
    B, D = c.shape
    n = w_ada.shape[1]
    return pl.pallas_call(
        _mod_kernel,
        grid=(n // D,),
        in_specs=[pl.BlockSpec((B, D), lambda j: (0, 0)),
                  pl.BlockSpec((D, D), lambda j: (0, j)),
                  pl.BlockSpec((1, D), lambda j: (0, j))],
        out_specs=pl.BlockSpec((B, D), lambda j: (0, j)),
        out_shape=jax.ShapeDtypeStruct((B, n), F32),
        compiler_params=_cparams(("arbitrary",)),
        name="mod",
    )(c, w_ada, b_ada.reshape(1, n))


def _shift_rows(cur, prev_row):
    rolled = pltpu.roll(cur, 1, 0)
    row = lax.broadcasted_iota(jnp.int32, cur.shape, 0)
    return jnp.where(row == 0, prev_row, rolled)


def _inproj_kernel(x_ref, xp_ref, mod_ref, g_ref, wrkv_ref, wqkv_ref, wgate_ref, murkv_ref, muwag_ref,
                   w1_ref, w2_ref, w0_ref, a1_ref, a2_ref, a0_ref, g1_ref, g2_ref,
                   rkv_ref, ld_ref, ai_ref, go_ref, qkv_ref, gate_ref, *, tiles_per_seq):
    first = (pl.program_id(0) % tiles_per_seq) == 0
    shift = mod_ref[0, 0:1, :]
    scale = mod_ref[0, 1:2, :]
    gain = g_ref[...]

    def norm_mod(xv):
        ms = jnp.mean(xv * xv, axis=-1, keepdims=True)
        return (xv * lax.rsqrt(ms + NORM_EPS) * gain) * (1.0 + scale) + shift

    h = norm_mod(x_ref[...])
    hp = jnp.where(first, 0.0, norm_mod(xp_ref[...]))
    hb = h.astype(BF16)

    p = _dg(hb, wrkv_ref[...])
    pp = _dg(hp.astype(BF16), wrkv_ref[...])[7:8, :]
    rkv_ref[...] = p + (_shift_rows(p, pp) - p) * murkv_ref[...]

    qkv_ref[...] = _dg(hb, wqkv_ref[...]).astype(BF16)
    gate_ref[...] = _sigmoid(_dg(hb, wgate_ref[...])).astype(BF16)

    xx = _shift_rows(h, hp[7:8, :]) - h
    xw = (h + xx * muwag_ref[0:1, :]).astype(BF16)
    xa = (h + xx * muwag_ref[1:2, :]).astype(BF16)
    xg = (h + xx * muwag_ref[2:3, :]).astype(BF16)
    u = w0_ref[...] + _dg(jnp.tanh(_dg(xw, w1_ref[...])).astype(BF16), w2_ref[...])
    w_log = _log_sigmoid(u) - 0.5
    ld_ref[...] = -jnp.exp(w_log)
    ai_ref[...] = _sigmoid(a0_ref[...] + _dg(_dg(xa, a1_ref[...]).astype(BF16), a2_ref[...]))
    go_ref[...] = _dg(_sigmoid(_dg(xg, g1_ref[...])).astype(BF16), g2_ref[...])


def _inproj_call(x2, mod3, norm_g, w_rkv, w_qkv, w_gate, mu_rkv, mu_wag, w1, w2, w0, a1, a2, a0, g1, g2,
                 *, seq, tm):
    M, D = x2.shape
    W3 = w_rkv.shape[1]
    W = W3 // 3
    tiles_per_seq = seq // tm
    const = lambda shape: pl.BlockSpec(shape, lambda i: (0,) * len(shape))
    rows = lambda n: pl.BlockSpec((tm, n), lambda i: (i, 0))
    return pl.pallas_call(
        functools.partial(_inproj_kernel, tiles_per_seq=tiles_per_seq),
        grid=(M // tm,),
        in_specs=[rows(D),
                  pl.BlockSpec((8, D), lambda i: (jnp.maximum(i * (tm // 8) - 1, 0), 0)),
                  pl.BlockSpec((1, mod3.shape[1], D), lambda i: (i // tiles_per_seq, 0, 0)),
                  const((1, D)), const(w_rkv.shape), const(w_qkv.shape), const(w_gate.shape),
                  const((1, W3)), const(mu_wag.shape),
                  const(w1.shape), const(w2.shape), const((1, W)),
                  const(a1.shape), const(a2.shape), const((1, W)),
                  const(g1.shape), const(g2.shape)],
        out_specs=[rows(W3), rows(W), rows(W), rows(W), rows(w_qkv.shape[1]), rows(w_gate.shape[1])],
        out_shape=[jax.ShapeDtypeStruct((M, W3), F32),
                   jax.ShapeDtypeStruct((M, W), F32),
                   jax.ShapeDtypeStruct((M, W), F32),
                   jax.ShapeDtypeStruct((M, W), F32),
                   jax.ShapeDtypeStruct((M, w_qkv.shape[1]), BF16),
                   jax.ShapeDtypeStruct((M, w_gate.shape[1]), BF16)],
        compiler_params=_cparams(("arbitrary",)),
        name="inproj",
    )(x2, x2, mod3, norm_g.reshape(1, D), w_rkv, w_qkv, w_gate, mu_rkv.reshape(1, W3), mu_wag,
      w1, w2, w0.reshape(1, W), a1, a2, a0.reshape(1, W), g1, g2)


def _rwkv_kernel(r_ref, k_ref, v_ref, ld_ref, ai_ref, g_ref, kk_ref, ka_ref, rk_ref, lw_ref, lb_ref,
                 y_ref, state_ref, *, chunk):
    C = chunk
    tb, W = r_ref.shape
    n_slab = W // LANES
    n_chunks = tb // C
    S2 = 2 * C
    units = [(p, c) for c in range(n_chunks) for p in range(n_slab)]

    @pl.when(pl.program_id(1) == 0)
    def _():
        state_ref[...] = jnp.zeros_like(state_ref)

    lane = lax.broadcasted_iota(jnp.int32, (1, LANES), 1)
    head0 = lane < HEAD_DIM
    li = lax.broadcasted_iota(jnp.int32, (LANES, LANES), 0)
    lj = lax.broadcasted_iota(jnp.int32, (LANES, LANES), 1)
    seg_ones = ((li // HEAD_DIM) == (lj // HEAD_DIM)).astype(BF16)
    ti = lax.broadcasted_iota(jnp.int32, (tb, tb), 0)
    tj = lax.broadcasted_iota(jnp.int32, (tb, tb), 1)
    cum_ones = ((ti // C == tj // C) & (tj <= ti)).astype(BF16)
    si = lax.broadcasted_iota(jnp.int32, (S2, S2), 0)
    sj = lax.broadcasted_iota(jnp.int32, (S2, S2), 1)
    same = (si // C) == (sj // C)
    strict = same & (sj < si)
    incl = same & (sj <= si)
    eye = si == sj
    level_masks = []
    bsz = 1
    while bsz < C:
        level_masks.append(((si // (2 * bsz)) == (sj // (2 * bsz)))
                           & (((si // bsz) % 2) == 1) & (((sj // bsz) % 2) == 0))
        bsz *= 2

    def seg_sum(x):
        return _mm(x, seg_ones, na=2)

    def stack(x):
        return jnp.concatenate([jnp.where(head0, x, 0.0), jnp.where(head0, 0.0, x)], axis=0)

    r, v, k2, e_in, a_t, r_t, b_t, k_t = ([] for _ in range(8))
    for p in range(n_slab):
        cols = slice(p * LANES, (p + 1) * LANES)
        ld = ld_ref[:, cols]
        ai = ai_ref[:, cols]
        k = k_ref[:, cols]
        kk = k * kk_ref[:, cols]
        kk = kk / jnp.maximum(jnp.sqrt(seg_sum(kk * kk)), 1e-12)
        cum = _mm(cum_ones, ld, nb=2)
        e_neg = jnp.exp(-cum)
        r.append(r_ref[:, cols])
        v.append(v_ref[:, cols])
        k2.append(k * (1.0 + (ai - 1.0) * ka_ref[:, cols]))
        e_in.append(jnp.exp(cum))
        a_t.append(-kk * jnp.exp(cum - ld))
        r_t.append(r[p] * e_in[p])
        b_t.append(kk * ai * e_neg)
        k_t.append(k2[p] * e_neg)

    rows = lambda c: slice(c * C, (c + 1) * C)
    p_end = [e_in[p][(c + 1) * C - 1:(c + 1) * C, :] for p, c in units]
    a_s = [stack(a_t[p][rows(c)]).astype(BF16) for p, c in units]
    r_s = [stack(r_t[p][rows(c)]) for p, c in units]
    v_s = [stack(v[p][rows(c)]).astype(BF16) for p, c in units]
    b_f = [stack(b_t[p][rows(c)]) for p, c in units]
    k_f = [stack(k_t[p][rows(c)]) for p, c in units]
    n = range(len(units))

    sc = [_dg(jnp.concatenate([a_s[u], r_s[u].astype(BF16)], axis=0),
              jnp.concatenate([b_f[u], k_f[u]], axis=0).astype(BF16), _NT) for u in n]
    a_ab = [jnp.where(strict, sc[u][:S2, :S2], 0.0) for u in n]
    a_ak = [jnp.where(strict, sc[u][:S2, S2:], 0.0).astype(BF16) for u in n]
    a_rb = [jnp.where(incl, sc[u][S2:, :S2], 0.0).astype(BF16) for u in n]
    a_rk = [jnp.where(incl, sc[u][S2:, S2:], 0.0).astype(BF16) for u in n]

    pinv = [jnp.where(eye, 1.0, jnp.where(level_masks[0], a_ab[u], 0.0)) for u in n]
    for mask in level_masks[1:]:
        pb = [pinv[u].astype(BF16) for u in n]
        xl = [_dg(pb[u], jnp.where(mask, a_ab[u], 0.0).astype(BF16)) for u in n]
        pinv = [pinv[u] + _dg(xl[u].astype(BF16), pb[u]) for u in n]

    av = [_dg(a_ak[u], v_s[u]) for u in n]
    ta = [_dg(pinv[u].astype(BF16), jnp.concatenate([a_s[u], av[u].astype(BF16)], axis=1)).astype(BF16)
          for u in n]
    rb = [_dg(a_rb[u], ta[u]) for u in n]
    rk = [_dg(a_rk[u], v_s[u]) for u in n]
    bt = [_dg((b_f[u] * p_end[u]).astype(BF16), ta[u], _TN) for u in n]
    kv = [_dg((k_f[u] * p_end[u]).astype(BF16), v_s[u], _TN) for u in n]
    r_h = [(r_s[u] + rb[u][:, :LANES]).astype(BF16) for u in n]
    y_h = [rb[u][:, LANES:] + rk[u] for u in n]
    m_mat = [(jnp.where(eye, p_end[u], 0.0) + bt[u][:, :LANES]).astype(BF16) for u in n]
    g_mat = [bt[u][:, LANES:] + kv[u] for u in n]

    state = [state_ref[p] for p in range(n_slab)]
    ys = [[] for _ in range(n_slab)]
    for u, (p, c) in enumerate(units):
        sb = state[p].astype(BF16)
        y_st = _dg(r_h[u], sb) + y_h[u]
        ys[p].append(y_st[:C] + y_st[C:])
        state[p] = _dg(m_mat[u], sb) + g_mat[u]

    inv_n = 1.0 / HEAD_DIM
    for p in range(n_slab):
        cols = slice(p * LANES, (p + 1) * LANES)
        state_ref[p] = state[p]
        y = jnp.concatenate(ys[p], axis=0) if n_chunks > 1 else ys[p][0]
        mean = seg_sum(y) * inv_n
        yc = y - mean
        var = seg_sum(yc * yc) * inv_n
        yn = yc * lax.rsqrt(var + LNX_EPS) * lw_ref[:, cols] + lb_ref[:, cols]
        bonus = seg_sum(r[p] * k2[p] * rk_ref[:, cols]) * v[p]
        y_ref[:, cols] = (yn + bonus) * g_ref[:, cols]


def _rwkv_call(rkv, ld, ai, g, k_k, k_a, r_k, lnx_w, lnx_b, *, batch, seq, tb):
    M, W = ld.shape
    steps = seq // tb
    tok = lambda off: pl.BlockSpec((tb, W), lambda b, s: (b * steps + s, off))
    par = pl.BlockSpec((1, W), lambda b, s: (0, 0))
    return pl.pallas_call(
        functools.partial(_rwkv_kernel, chunk=RWKV_CHUNK),
        grid=(batch, steps),
        in_specs=[tok(0), tok(1), tok(2), tok(0), tok(0), tok(0), par, par, par, par, par],
        out_specs=tok(0),
        out_shape=jax.ShapeDtypeStruct((M, W), F32),
        scratch_shapes=[pltpu.VMEM((W // LANES, LANES, LANES), F32)],
        compiler_params=_cparams(("arbitrary", "arbitrary")),
        name="rwkv",
    )(rkv, rkv, rkv, ld, ai, g, k_k.reshape(1, W), k_a.reshape(1, W), r_k.reshape(1, W),
      lnx_w.reshape(1, W), lnx_b.reshape(1, W))


def _stick_kernel(q_ref, k_ref, v_ref, o_ref, acc_ref, run_ref, *, tile, group):
    first = pl.program_id(2) * group
    lane = lax.broadcasted_iota(jnp.int32, (1, LANES), 1)
    head0 = lane < HEAD_DIM
    ti = lax.broadcasted_iota(jnp.int32, (tile, tile), 0)
    tj = lax.broadcasted_iota(jnp.int32, (tile, tile), 1)
    past = tj < ti
    ui = lax.broadcasted_iota(jnp.int32, (tile, tile + LANES), 0)
    uj = lax.broadcasted_iota(jnp.int32, (tile, tile + LANES), 1)
    rev_ones = ((ui > uj) | (uj >= tile)).astype(BF16)

    q = q_ref[...] * jnp.asarray(HEAD_DIM ** -0.5, BF16)
    zero = jnp.zeros_like(q)
    q_heads = (jnp.where(head0, q, zero), jnp.where(head0, zero, q))
    chains = [(s, h) for s in range(group) for h in range(2)]
    q_rows = [q_heads[h][s * tile:(s + 1) * tile] for s, h in chains]

    def visit(d, diagonal):
        kb, vb, live = [], [], []
        for s in range(group):
            jb = first + s - d
            start = pl.multiple_of(jnp.maximum(jb, 0) * tile, tile)
            kb.append(k_ref[pl.ds(start, tile), :])
            vb.append(v_ref[pl.ds(start, tile), :])
            live.append(jb >= 0)
        z = [_dg(q_rows[i], kb[s], _NT) for i, (s, h) in enumerate(chains)]
        ls = [_log_sigmoid(x) for x in z]
        lk = [ls[i] - z[i] for i in range(len(chains))]
        if diagonal:
            lk = [jnp.where(past, x, 0.0) for x in lk]
        cs = [_mm(x, rev_ones, na=2) for x in lk]
        worst = None
        for i, (s, h) in enumerate(chains):
            if diagonal:
                w = jnp.where(past, jnp.exp(ls[i] + cs[i][:, :tile]), 0.0)
                run = cs[i][:, tile:]
                acc_ref[i] = _dg(w.astype(BF16), vb[s])
            else:
                run = run_ref[i]
                w = jnp.where(live[s], jnp.exp(ls[i] + cs[i][:, :tile] + run), 0.0)
                run = run + cs[i][:, tile:]
                acc_ref[i] += _dg(w.astype(BF16), vb[s])
            run_ref[i] = run
            worst = run if worst is None else jnp.maximum(worst, run)
        return jnp.max(worst)

    def cond(carry):
        d, worst = carry
        return (d < first + group) & (worst > -EXP_UNDERFLOW)

    def body(carry):
        d, _ = carry
        return d + 1, visit(d, False)

    lax.while_loop(cond, body, (jnp.int32(1), visit(0, True)))
    for s in range(group):
        o_ref[s * tile:(s + 1) * tile, :] = jnp.where(head0, acc_ref[2 * s], acc_ref[2 * s + 1]).astype(o_ref.dtype)


def _stick_call(qkv, *, batch, seq, group):
    M, W3 = qkv.shape
    W = W3 // 3
    n_slab = W // LANES
    tile = SB_TILE
    rows = tile * group
    steps = seq // rows
    return pl.pallas_call(
        functools.partial(_stick_kernel, tile=tile, group=group),
        grid=(batch, n_slab, steps),
        in_specs=[pl.BlockSpec((rows, LANES), lambda b, p, i: (b * steps + i, p)),
                  pl.BlockSpec((seq, LANES), lambda b, p, i: (b, n_slab + p)),
                  pl.BlockSpec((seq, LANES), lambda b, p, i: (b, 2 * n_slab + p))],
        out_specs=pl.BlockSpec((rows, LANES), lambda b, p, i: (b * steps + i, p)),
        out_shape=jax.ShapeDtypeStruct((M, W), BF16),
        scratch_shapes=[pltpu.VMEM((2 * group, tile, LANES), F32), pltpu.VMEM((2 * group, tile, LANES), F32)],
        compiler_params=_cparams(("arbitrary", "arbitrary", "arbitrary")),
        name="stick",
    )(qkv, qkv, qkv)


def _merge_kernel(x_ref, ya_ref, yb_ref, gate_ref, mod_ref, wa_ref, wb_ref, wo_ref, o_ref):
    D = x_ref.shape[1]
    pa = _dg(ya_ref[...].astype(BF16), wa_ref[...])
    pb = _dg(yb_ref[...], wb_ref[...])
    merged = gate_ref[:, :D].astype(F32) * pa + gate_ref[:, D:].astype(F32) * pb
    o_ref[...] = x_ref[...] + mod_ref[0, 2:3, :] * _dg(merged.astype(BF16), wo_ref[...])


def _merge_call(x2, ya, yb, gates, mod3, w_pa, w_pb, w_out, *, seq, tm):
    M, D = x2.shape
    tiles_per_seq = seq // tm
    const = lambda shape: pl.BlockSpec(shape, lambda i: (0,) * len(shape))
    rows = lambda n: pl.BlockSpec((tm, n), lambda i: (i, 0))
    return pl.pallas_call(
        _merge_kernel,
        grid=(M // tm,),
        in_specs=[rows(D), rows(ya.shape[1]), rows(yb.shape[1]), rows(gates.shape[1]),
                  pl.BlockSpec((1, mod3.shape[1], D), lambda i: (i // tiles_per_seq, 0, 0)),
                  const(w_pa.shape), const(w_pb.shape), const(w_out.shape)],
        out_specs=rows(D),
        out_shape=jax.ShapeDtypeStruct((M, D), F32),
        compiler_params=_cparams(("arbitrary",)),
        name="merge",
    )(x2, ya, yb, gates, mod3, w_pa, w_pb, w_out)


def _ffn_kernel(x_ref, mod_ref, g2_ref, gf_ref, wg_ref, wu_ref, wd_ref, o_ref, h_ref, acc_ref, *, final_norm):
    j = pl.program_id(1)

    @pl.when(j == 0)
    def _():
        xv = x_ref[...]
        ms = jnp.mean(xv * xv, axis=-1, keepdims=True)
        y = xv * lax.rsqrt(ms + NORM_EPS) * g2_ref[...]
        h_ref[...] = (y * (1.0 + mod_ref[0, 4:5, :]) + mod_ref[0, 3:4, :]).astype(BF16)
        acc_ref[...] = jnp.zeros_like(acc_ref)

    hb = h_ref[...]
    ug = _dg(hb, wg_ref[...])
    uu = _dg(hb, wu_ref[...])
    act = (ug * _sigmoid(ug) * uu).astype(BF16)
    acc_ref[...] += _dg(act, wd_ref[...])

    @pl.when(j == pl.num_programs(1) - 1)
    def _():
        x2 = x_ref[...] + mod_ref[0, 5:6, :] * acc_ref[...]
        if final_norm:
            ms = jnp.mean(x2 * x2, axis=-1, keepdims=True)
            x2 = x2 * lax.rsqrt(ms + NORM_EPS) * gf_ref[...]
        o_ref[...] = x2


def _ffn_call(x1, mod3, norm2_g, final_g, w_g, w_u, w_d, *, seq, tm, tf, final_norm):
    M, D = x1.shape
    F = w_g.shape[1]
    tiles_per_seq = seq // tm
    return pl.pallas_call(
        functools.partial(_ffn_kernel, final_norm=final_norm),
        grid=(M // tm, F // tf),
        in_specs=[pl.BlockSpec((tm, D), lambda i, j: (i, 0)),
                  pl.BlockSpec((1, mod3.shape[1], D), lambda i, j: (i // tiles_per_seq, 0, 0)),
                  pl.BlockSpec((1, D), lambda i, j: (0, 0)),
                  pl.BlockSpec((1, D), lambda i, j: (0, 0)),
                  pl.BlockSpec((D, tf), lambda i, j: (0, j)),
                  pl.BlockSpec((D, tf), lambda i, j: (0, j)),
                  pl.BlockSpec((tf, D), lambda i, j: (j, 0))],
        out_specs=pl.BlockSpec((tm, D), lambda i, j: (i, 0)),
        out_shape=jax.ShapeDtypeStruct((M, D), F32),
        scratch_shapes=[pltpu.VMEM((tm, D), BF16), pltpu.VMEM((tm, D), F32)],
        compiler_params=_cparams(("arbitrary", "arbitrary")),
        name="ffn",
    )(x1, mod3, norm2_g.reshape(1, D), final_g.reshape(1, D), w_g, w_u, w_d)


def _tiles(seq):
    pick = lambda want: max(t for t in (8, 16, 32, 64, 128, 256, 512, 1024) if t <= want and seq % t == 0)
    return dict(inproj=pick(256), rwkv=pick(256), stick=pick(4 * SB_TILE), merge=pick(512), ffn=pick(512))


def kernel(x, c, w_ada, b_ada, norm1_g, w_in, mu_rkv, mu_wag, w0, w1, w2, a0, a1, a2, g1, g2, k_k, k_a, r_k,
           lnx_w, lnx_b, w_proj_a, w_proj_b, w_out, norm2_g, w_ffn_in, w_ffn_out, final_g):
    B, T, D = x.shape
    depth = w_ada.shape[0]
    W = w0.shape[1]
    S = w_proj_b.shape[1]
    F = w_ffn_out.shape[1]
    assert T % SB_TILE == 0 and W % LANES == 0 and S % LANES == 0
    tiles = _tiles(T)
    tf = F // 2 if (F // 2) % LANES == 0 else F
    x2 = x.reshape(B * T, D)
    for l in range(depth):
        mod3 = _mod_call(c, w_ada[l], b_ada[l]).reshape(B, 6, D)
        wl = w_in[l].astype(BF16)
        rkv, ld, ai, g, qkv, gates = _inproj_call(
            x2, mod3, norm1_g[l], wl[:, :3 * W], wl[:, 3 * W:3 * W + 3 * S], wl[:, 3 * W + 3 * S:],
            mu_rkv[l], mu_wag[l], w1[l].astype(BF16), w2[l].astype(BF16), w0[l],
            a1[l].astype(BF16), a2[l].astype(BF16), a0[l], g1[l].astype(BF16), g2[l].astype(BF16),
            seq=T, tm=tiles["inproj"])
        ya = _rwkv_call(rkv, ld, ai, g, k_k[l], k_a[l], r_k[l], lnx_w[l], lnx_b[l],
                        batch=B, seq=T, tb=tiles["rwkv"])
        yb = _stick_call(qkv, batch=B, seq=T, group=tiles["stick"] // SB_TILE)
        x1 = _merge_call(x2, ya, yb, gates, mod3, w_proj_a[l].astype(BF16), w_proj_b[l].astype(BF16),
                         w_out[l].astype(BF16), seq=T, tm=tiles["merge"])
        wf = w_ffn_in[l].astype(BF16)
        x2 = _ffn_call(x1, mod3, norm2_g[l], final_g, wf[:, :F], wf[:, F:], w_ffn_out[l].astype(BF16),
                       seq=T, tm=tiles["ffn"], tf=tf, final_norm=(l == depth - 1))
    return x2.reshape(B, T, D)
```

```python
import functools

import jax
import jax.numpy as jnp
from jax import lax
from jax.experimental import pallas as pl
from jax.experimental.pallas import tpu as pltpu

F32 = jnp.float32
BF16 = jnp.bfloat16

HEAD_DIM = 64
LANES = 128
NORM_EPS = 1e-6
LNX_EPS = 64e-5
RWKV_CHUNK = 64
SB_TILE = 128
EXP_UNDERFLOW = 104.0
VMEM_LIMIT = 56 * 1024 * 1024


def _cparams(sem):
    return pltpu.CompilerParams(dimension_semantics=sem, vmem_limit_bytes=VMEM_LIMIT)


def _const_spec(shape):
    return pl.BlockSpec(shape, lambda i: (0,) * len(shape), pipeline_mode=pl.Buffered(1))


_NN = (((1,), (0,)), ((), ()))
_NT = (((1,), (1,)), ((), ()))
_TN = (((0,), (0,)), ((), ()))


def _dg(a, b, dn=_NN):
    return lax.dot_general(a, b, dn, preferred_element_type=F32)


def _split(x, n):
    pieces, rem = [], x
    for i in range(n):
        p = rem.astype(BF16)
        pieces.append(p)
        if i + 1 < n:
            rem = rem - p.astype(F32)
    return pieces


def _mm(a, b, dn=_NN, na=1, nb=1):
    pa = _split(a, na) if a.dtype != BF16 else [a]
    pb = _split(b, nb) if b.dtype != BF16 else [b]
    order = max(len(pa), len(pb))
    acc = None
    for i, x in enumerate(pa):
        for j, y in enumerate(pb):
            if i + j < order:
                t = _dg(x, y, dn)
                acc = t if acc is None else acc + t
    return acc


def _sigmoid(x):
    return 1.0 / (1.0 + jnp.exp(-x))


def _neg_abs(x):
    bits = lax.bitcast_convert_type(x, jnp.uint32) | jnp.uint32(0x80000000)
    return lax.bitcast_convert_type(bits, F32)


def _log_sigmoid(x):
    return jnp.minimum(x, 0.0) - jnp.log(1.0 + jnp.exp(_neg_abs(x)))


def _mod_kernel(c_ref, w_ref, b_ref, o_ref):
    c = c_ref[...]
    c_act = c * _sigmoid(c)
    o_ref[...] = _mm(c_act, w_ref[...], na=3, nb=3) + b_ref[...]


def _mod_call(c, w_ada, b_ada):
    B, D = c.shape
    n = w_ada.shape[1]
    return pl.pallas_call(
        _mod_kernel,
        grid=(n // D,),
        in_specs=[pl.BlockSpec((B, D), lambda j: (0, 0)),
                  pl.BlockSpec((D, D), lambda j: (0, j)),
                  pl.BlockSpec((1, D), lambda j: (0, j))],
        out_specs=pl.BlockSpec((B, D), lambda j: (0, j)),
        out_shape=jax.ShapeDtypeStruct((B, n), F32),
        compiler_params=_cparams(("arbitrary",)),
        name="mod",
    )(c, w_ada, b_ada.reshape(1, n))


def _shift_rows(cur, prev_row):
    rolled = pltpu.roll(cur, 1, 0)
    row = lax.broadcasted_iota(jnp.int32, cur.shape, 0)
    return jnp.where(row == 0, prev_row, rolled)


def _inproj_kernel(x_ref, xp_ref, mod_ref, g_ref, wrkv_ref, wk_ref, wqvt_ref, wgate_ref, murkv_ref, muwag_ref,
                   w1_ref, w2_ref, w0_ref, a1_ref, a2_ref, a0_ref, g1_ref, g2_ref,
                   rkv_ref, ld_ref, ai_ref, go_ref, ksb_ref, qt_ref, vt_ref, gate_ref, *, tiles_per_seq, sub):
    first = (pl.program_id(0) % tiles_per_seq) == 0
    shift = mod_ref[0, 0:1, :]
    scale = mod_ref[0, 1:2, :]
    gain = g_ref[...]

    def norm_mod(xv):
        ms = jnp.mean(xv * xv, axis=-1, keepdims=True)
        return (xv * lax.rsqrt(ms + NORM_EPS) * gain) * (1.0 + scale) + shift

    hp = jnp.where(first, 0.0, norm_mod(xp_ref[...]))
    h_prev = hp[7:8, :]
    p_prev = _dg(hp.astype(BF16), wrkv_ref[...])[7:8, :]
    for s in range(x_ref.shape[0] // sub):
        rows = slice(s * sub, (s + 1) * sub)
        h = norm_mod(x_ref[rows, :])
        hb = h.astype(BF16)

        p = _dg(hb, wrkv_ref[...])
        rkv_ref[rows, :] = p + (_shift_rows(p, p_prev) - p) * murkv_ref[...]
        p_prev = p[sub - 1:sub, :]

        ksb_ref[rows, :] = _dg(hb, wk_ref[...]).astype(BF16)
        qv = _dg(wqvt_ref[...], hb, _NT)
        n_feat = qt_ref.shape[1]
        for blk in range(sub // SB_TILE):
            cols = slice(blk * SB_TILE, (blk + 1) * SB_TILE)
            qt_ref[s * (sub // SB_TILE) + blk] = (qv[:n_feat, cols] * (HEAD_DIM ** -0.5)).astype(BF16)
            vt_ref[s * (sub // SB_TILE) + blk] = qv[n_feat:, cols].astype(BF16)
        gate_ref[rows, :] = _sigmoid(_dg(hb, wgate_ref[...])).astype(BF16)

        xx = _shift_rows(h, h_prev) - h
        h_prev = h[sub - 1:sub, :]
        xw = (h + xx * muwag_ref[0:1, :]).astype(BF16)
        xa = (h + xx * muwag_ref[1:2, :]).astype(BF16)
        xg = (h + xx * muwag_ref[2:3, :]).astype(BF16)
        u = w0_ref[...] + _dg(jnp.tanh(_dg(xw, w1_ref[...])).astype(BF16), w2_ref[...])
        w_log = _log_sigmoid(u) - 0.5
        ld_ref[rows, :] = -jnp.exp(w_log)
        ai_ref[rows, :] = _sigmoid(a0_ref[...] + _dg(_dg(xa, a1_ref[...]).astype(BF16), a2_ref[...]))
        go_ref[rows, :] = _dg(_sigmoid(_dg(xg, g1_ref[...])).astype(BF16), g2_ref[...])


def _inproj_call(x2, mod3, norm_g, w_rkv, w_k, w_qvt, w_gate, mu_rkv, mu_wag, w1, w2, w0, a1, a2, a0, g1, g2,
                 *, seq, tm, sub):
    M, D = x2.shape
    W3 = w_rkv.shape[1]
    W = W3 // 3
    S = w_k.shape[1]
    tiles_per_seq = seq // tm
    const = _const_spec
    rows = lambda n: pl.BlockSpec((tm, n), lambda i: (i, 0))
    feat_major = pl.BlockSpec((tm // SB_TILE, S, SB_TILE), lambda i: (i, 0, 0))
    return pl.pallas_call(
        functools.partial(_inproj_kernel, tiles_per_seq=tiles_per_seq, sub=sub),
        grid=(M // tm,),
        in_specs=[rows(D),
                  pl.BlockSpec((8, D), lambda i: (jnp.maximum(i * (tm // 8) - 1, 0), 0)),
                  pl.BlockSpec((1, mod3.shape[1], D), lambda i: (i // tiles_per_seq, 0, 0)),
                  const((1, D)), const(w_rkv.shape), const(w_k.shape), const(w_qvt.shape), const(w_gate.shape),
                  const((1, W3)), const(mu_wag.shape),
                  const(w1.shape), const(w2.shape), const((1, W)),
                  const(a1.shape), const(a2.shape), const((1, W)),
                  const(g1.shape), const(g2.shape)],
        out_specs=[rows(W3), rows(W), rows(W), rows(W), rows(S), feat_major, feat_major, rows(w_gate.shape[1])],
        out_shape=[jax.ShapeDtypeStruct((M, W3), F32),
                   jax.ShapeDtypeStruct((M, W), F32),
                   jax.ShapeDtypeStruct((M, W), F32),
                   jax.ShapeDtypeStruct((M, W), F32),
                   jax.ShapeDtypeStruct((M, S), BF16),
                   jax.ShapeDtypeStruct((M // SB_TILE, S, SB_TILE), BF16),
                   jax.ShapeDtypeStruct((M // SB_TILE, S, SB_TILE), BF16),
                   jax.ShapeDtypeStruct((M, w_gate.shape[1]), BF16)],
        compiler_params=_cparams(("arbitrary",)),
        name="inproj",
    )(x2, x2, mod3, norm_g.reshape(1, D), w_rkv, w_k, w_qvt, w_gate, mu_rkv.reshape(1, W3), mu_wag,
      w1, w2, w0.reshape(1, W), a1, a2, a0.reshape(1, W), g1, g2)


def _rwkv_kernel(r_ref, k_ref, v_ref, ld_ref, ai_ref, g_ref, kk_ref, ka_ref, rk_ref, lw_ref, lb_ref,
                 y_ref, state_ref, *, chunk):
    C = chunk
    tb, W = r_ref.shape
    n_slab = W // LANES
    n_chunks = tb // C
    S2 = 2 * C
    units = [(p, c) for c in range(n_chunks) for p in range(n_slab)]

    @pl.when(pl.program_id(1) == 0)
    def _():
        state_ref[...] = jnp.zeros_like(state_ref)

    lane = lax.broadcasted_iota(jnp.int32, (1, LANES), 1)
    head0 = lane < HEAD_DIM
    li = lax.broadcasted_iota(jnp.int32, (LANES, LANES), 0)
    lj = lax.broadcasted_iota(jnp.int32, (LANES, LANES), 1)
    seg_ones = ((li // HEAD_DIM) == (lj // HEAD_DIM)).astype(BF16)
    ti = lax.broadcasted_iota(jnp.int32, (tb, tb), 0)
    tj = lax.broadcasted_iota(jnp.int32, (tb, tb), 1)
    cum_ones = ((ti // C == tj // C) & (tj <= ti)).astype(BF16)
    si = lax.broadcasted_iota(jnp.int32, (S2, S2), 0)
    sj = lax.broadcasted_iota(jnp.int32, (S2, S2), 1)
    same = (si // C) == (sj // C)
    strict = same & (sj < si)
    incl = same & (sj <= si)
    eye = si == sj
    level_masks = []
    bsz = 1
    while bsz < C:
        level_masks.append(((si // (2 * bsz)) == (sj // (2 * bsz)))
                           & (((si // bsz) % 2) == 1) & (((sj // bsz) % 2) == 0))
        bsz *= 2

    def seg_sum(x):
        return _mm(x, seg_ones, na=2)

    def stack(x):
        return jnp.concatenate([jnp.where(head0, x, 0.0), jnp.where(head0, 0.0, x)], axis=0)

    r, v, k2, e_in, a_t, r_t, b_t, k_t = ([] for _ in range(8))
    for p in range(n_slab):
        cols = slice(p * LANES, (p + 1) * LANES)
        ld = ld_ref[:, cols]
        ai = ai_ref[:, cols]
        k = k_ref[:, cols]
        kk = k * kk_ref[:, cols]
        kk = kk / jnp.maximum(jnp.sqrt(seg_sum(kk * kk)), 1e-12)
        cum = _mm(cum_ones, ld, nb=2)
        e_neg = jnp.exp(-cum)
        r.append(r_ref[:, cols])
        v.append(v_ref[:, cols])
        k2.append(k * (1.0 + (ai - 1.0) * ka_ref[:, cols]))
        e_in.append(jnp.exp(cum))
        a_t.append(-kk * jnp.exp(cum - ld))
        r_t.append(r[p] * e_in[p])
        b_t.append(kk * ai * e_neg)
        k_t.append(k2[p] * e_neg)

    rows = lambda c: slice(c * C, (c + 1) * C)
    p_end = [e_in[p][(c + 1) * C - 1:(c + 1) * C, :] for p, c in units]
    a_s = [stack(a_t[p][rows(c)]).astype(BF16) for p, c in units]
    r_s = [stack(r_t[p][rows(c)]) for p, c in units]
    v_s = [stack(v[p][rows(c)]).astype(BF16) for p, c in units]
    b_f = [stack(b_t[p][rows(c)]) for p, c in units]
    k_f = [stack(k_t[p][rows(c)]) for p, c in units]
    n = range(len(units))

    sc = [_dg(jnp.concatenate([a_s[u], r_s[u].astype(BF16)], axis=0),
              jnp.concatenate([b_f[u], k_f[u]], axis=0).astype(BF16), _NT) for u in n]
    a_ab = [jnp.where(strict, sc[u][:S2, :S2], 0.0) for u in n]
    a_ak = [jnp.where(strict, sc[u][:S2, S2:], 0.0).astype(BF16) for u in n]
    a_rb = [jnp.where(incl, sc[u][S2:, :S2], 0.0).astype(BF16) for u in n]
    a_rk = [jnp.where(incl, sc[u][S2:, S2:], 0.0).astype(BF16) for u in n]

    pinv = [jnp.where(eye, 1.0, jnp.where(level_masks[0], a_ab[u], 0.0)) for u in n]
    for mask in level_masks[1:]:
        pb = [pinv[u].astype(BF16) for u in n]
        xl = [_dg(pb[u], jnp.where(mask, a_ab[u], 0.0).astype(BF16)) for u in n]
        pinv = [pinv[u] + _dg(xl[u].astype(BF16), pb[u]) for u in n]

    av = [_dg(a_ak[u], v_s[u]) for u in n]
    ta = [_dg(pinv[u].astype(BF16), jnp.concatenate([a_s[u], av[u].astype(BF16)], axis=1)).astype(BF16)
          for u in n]
    rb = [_dg(a_rb[u], ta[u]) for u in n]
    rk = [_dg(a_rk[u], v_s[u]) for u in n]
    bt = [_dg((b_f[u] * p_end[u]).astype(BF16), ta[u], _TN) for u in n]
    kv = [_dg((k_f[u] * p_end[u]).astype(BF16), v_s[u], _TN) for u in n]
    r_h = [(r_s[u] + rb[u][:, :LANES]).astype(BF16) for u in n]
    y_h = [rb[u][:, LANES:] + rk[u] for u in n]
    m_mat = [(jnp.where(eye, p_end[u], 0.0) + bt[u][:, :LANES]).astype(BF16) for u in n]
    g_mat = [bt[u][:, LANES:] + kv[u] for u in n]

    state = [state_ref[p] for p in range(n_slab)]
    ys = [[] for _ in range(n_slab)]
    for u, (p, c) in enumerate(units):
        sb = state[p].astype(BF16)
        y_st = _dg(r_h[u], sb) + y_h[u]
        ys[p].append(y_st[:C] + y_st[C:])
        state[p] = _dg(m_mat[u], sb) + g_mat[u]

    inv_n = 1.0 / HEAD_DIM
    for p in range(n_slab):
        cols = slice(p * LANES, (p + 1) * LANES)
        state_ref[p] = state[p]
        y = jnp.concatenate(ys[p], axis=0) if n_chunks > 1 else ys[p][0]
        mean = seg_sum(y) * inv_n
        yc = y - mean
        var = seg_sum(yc * yc) * inv_n
        yn = yc * lax.rsqrt(var + LNX_EPS) * lw_ref[:, cols] + lb_ref[:, cols]
        bonus = seg_sum(r[p] * k2[p] * rk_ref[:, cols]) * v[p]
        y_ref[:, cols] = (yn + bonus) * g_ref[:, cols]


def _rwkv_call(rkv, ld, ai, g, k_k, k_a, r_k, lnx_w, lnx_b, *, batch, seq, tb):
    M, W = ld.shape
    steps = seq // tb
    tok = lambda off: pl.BlockSpec((tb, W), lambda b, s: (b * steps + s, off))
    par = pl.BlockSpec((1, W), lambda b, s: (0, 0))
    return pl.pallas_call(
        functools.partial(_rwkv_kernel, chunk=RWKV_CHUNK),
        grid=(batch, steps),
        in_specs=[tok(0), tok(1), tok(2), tok(0), tok(0), tok(0), par, par, par, par, par],
        out_specs=tok(0),
        out_shape=jax.ShapeDtypeStruct((M, W), F32),
        scratch_shapes=[pltpu.VMEM((W // LANES, LANES, LANES), F32)],
        compiler_params=_cparams(("arbitrary", "arbitrary")),
        name="rwkv",
    )(rkv, rkv, rkv, ld, ai, g, k_k.reshape(1, W), k_a.reshape(1, W), r_k.reshape(1, W),
      lnx_w.reshape(1, W), lnx_b.reshape(1, W))


def _stick_kernel(qt_ref, k_ref, vt_ref, o_ref, acc_ref, run_ref, *, tile, group):
    first = pl.program_id(2) * group
    si = lax.broadcasted_iota(jnp.int32, (tile, 2 * tile), 0)
    ti = lax.broadcasted_iota(jnp.int32, (tile, 2 * tile), 1) % tile
    past = si < ti
    ui = lax.broadcasted_iota(jnp.int32, (tile, tile), 0)
    uj = lax.broadcasted_iota(jnp.int32, (tile, tile), 1)
    later_ones = (uj > ui).astype(BF16)
    feat0 = lax.broadcasted_iota(jnp.int32, (LANES, tile), 0) < HEAD_DIM
    zero = jnp.zeros((LANES, tile), BF16)
    q_pair = [jnp.concatenate([jnp.where(feat0, qt_ref[s], zero), jnp.where(feat0, zero, qt_ref[s])], axis=1)
              for s in range(group)]
    n = range(group)

    def visit(d, diagonal):
        kb, vt = [], []
        for s in n:
            jb = first + s - d
            blk = jnp.maximum(jb, 0)
            kb.append(k_ref[pl.ds(pl.multiple_of(blk * tile, tile), tile), :])
            vt.append(vt_ref[blk] if diagonal else jnp.where(jb >= 0, vt_ref[blk], zero))
        z = [_dg(kb[s], q_pair[s]) for s in n]
        ls = [_log_sigmoid(x) for x in z]
        lk = [ls[s] - z[s] for s in n]
        if diagonal:
            lk = [jnp.where(past, x, 0.0) for x in lk]
        later = []
        for s in n:
            halves = []
            for h in range(2):
                x = lk[s][:, h * tile:(h + 1) * tile]
                hi = x.astype(BF16)
                lo = (x - hi.astype(F32)).astype(BF16)
                both = _dg(later_ones, jnp.concatenate([hi, lo], axis=1))
                halves.append(both[:, :tile] + both[:, tile:])
            later.append(jnp.concatenate(halves, axis=1))
        worst = None
        for s in n:
            total = later[s][0:1, :] + lk[s][0:1, :]
            if diagonal:
                w = jnp.where(past, jnp.exp(ls[s] + later[s]), 0.0)
                run = total
            else:
                run = run_ref[s]
                w = jnp.exp(ls[s] + later[s] + run)
                run = run + total
            pv = _dg(vt[s], w.astype(BF16))
            pv = jnp.concatenate([pv[:HEAD_DIM, :tile], pv[HEAD_DIM:, tile:]], axis=0)
            acc_ref[s] = pv if diagonal else acc_ref[s] + pv
            run_ref[s] = run
            worst = run if worst is None else jnp.maximum(worst, run)
        return jnp.max(worst)

    def cond(carry):
        d, worst = carry
        return (d < first + group) & (worst > -EXP_UNDERFLOW)

    def body(carry):
        d, _ = carry
        return d + 1, visit(d, False)

    lax.while_loop(cond, body, (jnp.int32(1), visit(0, True)))
    for s in n:
        o_ref[s * tile:(s + 1) * tile, :] = acc_ref[s].T.astype(o_ref.dtype)


def _stick_call(ksb, qt, vt, *, batch, seq, group):
    M, W = ksb.shape
    n_slab = W // LANES
    tile = SB_TILE
    rows = tile * group
    steps = seq // rows
    blocks = seq // tile
    return pl.pallas_call(
        functools.partial(_stick_kernel, tile=tile, group=group),
        grid=(batch, n_slab, steps),
        in_specs=[pl.BlockSpec((group, LANES, tile), lambda b, p, i: (b * steps + i, p, 0)),
                  pl.BlockSpec((seq, LANES), lambda b, p, i: (b, p)),
                  pl.BlockSpec((blocks, LANES, tile), lambda b, p, i: (b, p, 0))],
        out_specs=pl.BlockSpec((rows, LANES), lambda b, p, i: (b * steps + i, p)),
        out_shape=jax.ShapeDtypeStruct((M, W), BF16),
        scratch_shapes=[pltpu.VMEM((group, LANES, tile), F32), pltpu.VMEM((group, 1, 2 * tile), F32)],
        compiler_params=_cparams(("arbitrary", "arbitrary", "arbitrary")),
        name="stick",
    )(qt, ksb, vt)


def _post_kernel(x_ref, ya_ref, yb_ref, gate_ref, mod_ref, wa_ref, wb_ref, wo_ref, g2_ref, gf_ref,
                 wg_ref, wu_ref, wd_ref, o_ref, *, sub, final_norm):
    D = x_ref.shape[1]
    gt1, sh2, sc2, gt2 = (mod_ref[0, j:j + 1, :] for j in (2, 3, 4, 5))
    for s in range(x_ref.shape[0] // sub):
        rows = slice(s * sub, (s + 1) * sub)
        pa = _dg(ya_ref[rows, :].astype(BF16), wa_ref[...])
        pb = _dg(yb_ref[rows, :], wb_ref[...])
        merged = gate_ref[rows, :D].astype(F32) * pa + gate_ref[rows, D:].astype(F32) * pb
        x1 = x_ref[rows, :] + gt1 * _dg(merged.astype(BF16), wo_ref[...])

        ms = jnp.mean(x1 * x1, axis=-1, keepdims=True)
        hb = ((x1 * lax.rsqrt(ms + NORM_EPS) * g2_ref[...]) * (1.0 + sc2) + sh2).astype(BF16)
        ug = _dg(hb, wg_ref[...])
        uu = _dg(hb, wu_ref[...])
        act = (ug * _sigmoid(ug) * uu).astype(BF16)
        x2 = x1 + gt2 * _dg(act, wd_ref[...])
        if final_norm:
            ms = jnp.mean(x2 * x2, axis=-1, keepdims=True)
            x2 = x2 * lax.rsqrt(ms + NORM_EPS) * gf_ref[...]
        o_ref[rows, :] = x2


def _post_call(x2, ya, yb, gates, mod3, w_pa, w_pb, w_out, norm2_g, final_g, w_g, w_u, w_d,
               *, seq, tm, sub, final_norm):
    M, D = x2.shape
    tiles_per_seq = seq // tm
    const = _const_spec
    rows = lambda n: pl.BlockSpec((tm, n), lambda i: (i, 0))
    return pl.pallas_call(
        functools.partial(_post_kernel, sub=sub, final_norm=final_norm),
        grid=(M // tm,),
        in_specs=[rows(D), rows(ya.shape[1]), rows(yb.shape[1]), rows(gates.shape[1]),
                  pl.BlockSpec((1, mod3.shape[1], D), lambda i: (i // tiles_per_seq, 0, 0)),
                  const(w_pa.shape), const(w_pb.shape), const(w_out.shape), const((1, D)), const((1, D)),
                  const(w_g.shape), const(w_u.shape), const(w_d.shape)],
        out_specs=rows(D),
        out_shape=jax.ShapeDtypeStruct((M, D), F32),
        compiler_params=_cparams(("arbitrary",)),
        name="post",
    )(x2, ya, yb, gates, mod3, w_pa, w_pb, w_out, norm2_g.reshape(1, D), final_g.reshape(1, D), w_g, w_u, w_d)


def _tiles(seq):
    pick = lambda want: max(t for t in (8, 16, 32, 64, 128, 256, 512, 1024) if t <= want and seq % t == 0)
    return dict(inproj=pick(512), rwkv=pick(256), stick=pick(8 * SB_TILE), post=pick(512), sub=pick(512))


def kernel(x, c, w_ada, b_ada, norm1_g, w_in, mu_rkv, mu_wag, w0, w1, w2, a0, a1, a2, g1, g2, k_k, k_a, r_k,
           lnx_w, lnx_b, w_proj_a, w_proj_b, w_out, norm2_g, w_ffn_in, w_ffn_out, final_g):
    B, T, D = x.shape
    depth = w_ada.shape[0]
    W = w0.shape[1]
    S = w_proj_b.shape[1]
    F = w_ffn_out.shape[1]
    assert T % SB_TILE == 0 and W % LANES == 0 and S % LANES == 0
    tiles = _tiles(T)
    x2 = x.reshape(B * T, D)
    for l in range(depth):
        mod3 = _mod_call(c, w_ada[l], b_ada[l]).reshape(B, 6, D)
        wl = w_in[l].astype(BF16)
        w_q, w_k, w_v = (wl[:, 3 * W + j * S:3 * W + (j + 1) * S] for j in range(3))
        rkv, ld, ai, g, ksb, qt, vt, gates = _inproj_call(
            x2, mod3, norm1_g[l], wl[:, :3 * W], w_k, jnp.concatenate([w_q, w_v], axis=1).T,
            wl[:, 3 * W + 3 * S:],
            mu_rkv[l], mu_wag[l], w1[l].astype(BF16), w2[l].astype(BF16), w0[l],
            a1[l].astype(BF16), a2[l].astype(BF16), a0[l], g1[l].astype(BF16), g2[l].astype(BF16),
            seq=T, tm=tiles["inproj"], sub=tiles["sub"])
        ya = _rwkv_call(rkv, ld, ai, g, k_k[l], k_a[l], r_k[l], lnx_w[l], lnx_b[l],
                        batch=B, seq=T, tb=tiles["rwkv"])
        yb = _stick_call(ksb, qt, vt, batch=B, seq=T, group=tiles["stick"] // SB_TILE)
        wf = w_ffn_in[l].astype(BF16)
        x2 = _post_call(x2, ya, yb, gates, mod3, w_proj_a[l].astype(BF16), w_proj_b[l].astype(BF16),
                        w_out[l].astype(BF16), norm2_g[l], final_g, wf[:, :F], wf[:, F:],
                        w_ffn_out[l].astype(BF16), seq=T, tm=tiles["post"], sub=tiles["sub"],
                        final_norm=(l == depth - 1))
    return x2.reshape(B, T, D)
```

```python
import functools

import jax
import jax.numpy as jnp
from jax import lax
from jax.experimental import pallas as pl
from jax.experimental.pallas import tpu as pltpu

F32 = jnp.float32
BF16 = jnp.bfloat16

HEAD_DIM = 64
LANES = 128
NORM_EPS = 1e-6
LNX_EPS = 64e-5
RWKV_CHUNK = 64
SB_TILE = 128
SB_GROUP = 8
SB_FIXED_VISITS = 4
SB_VISITS_PER_TEST = 2
EXP_UNDERFLOW = 104.0
VMEM_LIMIT = 56 * 1024 * 1024


def _cparams(sem):
    return pltpu.CompilerParams(dimension_semantics=sem, vmem_limit_bytes=VMEM_LIMIT)


def _const_spec(shape):
    return pl.BlockSpec(shape, lambda i: (0,) * len(shape), pipeline_mode=pl.Buffered(1))


_NN = (((1,), (0,)), ((), ()))
_NT = (((1,), (1,)), ((), ()))
_TN = (((0,), (0,)), ((), ()))


def _dg(a, b, dn=_NN):
    return lax.dot_general(a, b, dn, preferred_element_type=F32)


def _split(x, n):
    pieces, rem = [], x
    for i in range(n):
        p = rem.astype(BF16)
        pieces.append(p)
        if i + 1 < n:
            rem = rem - p.astype(F32)
    return pieces


def _mm(a, b, dn=_NN, na=1, nb=1):
    pa = _split(a, na) if a.dtype != BF16 else [a]
    pb = _split(b, nb) if b.dtype != BF16 else [b]
    order = max(len(pa), len(pb))
    acc = None
    for i, x in enumerate(pa):
        for j, y in enumerate(pb):
            if i + j < order:
                t = _dg(x, y, dn)
                acc = t if acc is None else acc + t
    return acc


def _sigmoid(x):
    return 1.0 / (1.0 + jnp.exp(-x))


def _log_sigmoid(x):
    return jnp.minimum(x, 0.0) - jnp.log(1.0 + jnp.exp(-jnp.abs(x)))


def _mod_kernel(c_ref, w_ref, b_ref, o_ref):
    c = c_ref[...]
    c_act = c * _sigmoid(c)
    o_ref[...] = _mm(c_act, w_ref[...], na=3, nb=3) + b_ref[...]


def _mod_call(c, w_ada, b_ada):
    B, D = c.shape
    n = w_ada.shape[1]
    return pl.pallas_call(
        _mod_kernel,
        grid=(n // D,),
        in_specs=[pl.BlockSpec((B, D), lambda j: (0, 0)),
                  pl.BlockSpec((D, D), lambda j: (0, j)),
                  pl.BlockSpec((1, D), lambda j: (0, j))],
        out_specs=pl.BlockSpec((B, D), lambda j: (0, j)),
        out_shape=jax.ShapeDtypeStruct((B, n), F32),
        compiler_params=_cparams(("arbitrary",)),
        name="mod",
    )(c, w_ada, b_ada.reshape(1, n))


def _shift_rows(cur, prev_row):
    rolled = pltpu.roll(cur, 1, 0)
    row = lax.broadcasted_iota(jnp.int32, cur.shape, 0)
    return jnp.where(row == 0, prev_row, rolled)


def _inproj_kernel(x_ref, xp_ref, mod_ref, g_ref, wrkv_ref, wk_ref, wqvt_ref, wgate_ref, murkv_ref, muwag_ref,
                   w1_ref, w2_ref, w0_ref, a1_ref, a2_ref, a0_ref, g1_ref, g2_ref,
                   rkv_ref, ld_ref, ai_ref, go_ref, ksb_ref, qt_ref, vt_ref, gate_ref, *, tiles_per_seq):
    first = (pl.program_id(0) % tiles_per_seq) == 0
    shift = mod_ref[0, 0:1, :]
    scale = mod_ref[0, 1:2, :]
    gain = g_ref[...]

    def norm_mod(xv):
        ms = jnp.mean(xv * xv, axis=-1, keepdims=True)
        return (xv * lax.rsqrt(ms + NORM_EPS) * gain) * (1.0 + scale) + shift

    hp = jnp.where(first, 0.0, norm_mod(xp_ref[...]))
    h = norm_mod(x_ref[...])
    hb = h.astype(BF16)

    p = _dg(hb, wrkv_ref[...])
    p_prev = _dg(hp.astype(BF16), wrkv_ref[...])[7:8, :]
    rkv_ref[...] = p + (_shift_rows(p, p_prev) - p) * murkv_ref[...]

    ksb_ref[...] = _dg(hb, wk_ref[...]).astype(BF16)
    qv = _dg(wqvt_ref[...], hb, _NT)
    n_feat = qt_ref.shape[1]
    for blk in range(x_ref.shape[0] // SB_TILE):
        cols = slice(blk * SB_TILE, (blk + 1) * SB_TILE)
        qt_ref[blk] = (qv[:n_feat, cols] * (HEAD_DIM ** -0.5)).astype(BF16)
        vt_ref[blk] = qv[n_feat:, cols].astype(BF16)
    gate_ref[...] = _sigmoid(_dg(hb, wgate_ref[...])).astype(BF16)

    xx = _shift_rows(h, hp[7:8, :]) - h
    xw = (h + xx * muwag_ref[0:1, :]).astype(BF16)
    xa = (h + xx * muwag_ref[1:2, :]).astype(BF16)
    xg = (h + xx * muwag_ref[2:3, :]).astype(BF16)
    u = w0_ref[...] + _dg(jnp.tanh(_dg(xw, w1_ref[...])).astype(BF16), w2_ref[...])
    w_log = _log_sigmoid(u) - 0.5
    ld_ref[...] = -jnp.exp(w_log)
    ai_ref[...] = _sigmoid(a0_ref[...] + _dg(_dg(xa, a1_ref[...]).astype(BF16), a2_ref[...]))
    go_ref[...] = _dg(_sigmoid(_dg(xg, g1_ref[...])).astype(BF16), g2_ref[...])


def _inproj_call(x2, mod3, norm_g, w_rkv, w_k, w_qvt, w_gate, mu_rkv, mu_wag, w1, w2, w0, a1, a2, a0, g1, g2,
                 *, seq, tm):
    M, D = x2.shape
    W3 = w_rkv.shape[1]
    W = W3 // 3
    S = w_k.shape[1]
    tiles_per_seq = seq // tm
    const = _const_spec
    rows = lambda n: pl.BlockSpec((tm, n), lambda i: (i, 0))
    feat_major = pl.BlockSpec((tm // SB_TILE, S, SB_TILE), lambda i: (i, 0, 0))
    return pl.pallas_call(
        functools.partial(_inproj_kernel, tiles_per_seq=tiles_per_seq),
        grid=(M // tm,),
        in_specs=[rows(D),
                  pl.BlockSpec((8, D), lambda i: (jnp.maximum(i * (tm // 8) - 1, 0), 0)),
                  pl.BlockSpec((1, mod3.shape[1], D), lambda i: (i // tiles_per_seq, 0, 0)),
                  const((1, D)), const(w_rkv.shape), const(w_k.shape), const(w_qvt.shape), const(w_gate.shape),
                  const((1, W3)), const(mu_wag.shape),
                  const(w1.shape), const(w2.shape), const((1, W)),
                  const(a1.shape), const(a2.shape), const((1, W)),
                  const(g1.shape), const(g2.shape)],
        out_specs=[rows(W3), rows(W), rows(W), rows(W), rows(S), feat_major, feat_major, rows(w_gate.shape[1])],
        out_shape=[jax.ShapeDtypeStruct((M, W3), F32),
                   jax.ShapeDtypeStruct((M, W), F32),
                   jax.ShapeDtypeStruct((M, W), F32),
                   jax.ShapeDtypeStruct((M, W), F32),
                   jax.ShapeDtypeStruct((M, S), BF16),
                   jax.ShapeDtypeStruct((M // SB_TILE, S, SB_TILE), BF16),
                   jax.ShapeDtypeStruct((M // SB_TILE, S, SB_TILE), BF16),
                   jax.ShapeDtypeStruct((M, w_gate.shape[1]), BF16)],
        compiler_params=_cparams(("arbitrary",)),
        name="inproj",
    )(x2, x2, mod3, norm_g.reshape(1, D), w_rkv, w_k, w_qvt, w_gate, mu_rkv.reshape(1, W3), mu_wag,
      w1, w2, w0.reshape(1, W), a1, a2, a0.reshape(1, W), g1, g2)


def _rwkv_kernel(r_ref, k_ref, v_ref, ld_ref, ai_ref, g_ref, kk_ref, ka_ref, rk_ref, lw_ref, lb_ref,
                 y_ref, state_ref, *, chunk):
    C = chunk
    tb, W = r_ref.shape
    n_slab = W // LANES
    n_chunks = tb // C
    S2 = 2 * C
    units = [(p, c) for c in range(n_chunks) for p in range(n_slab)]

    @pl.when(pl.program_id(1) == 0)
    def _():
        state_ref[...] = jnp.zeros_like(state_ref)

    lane = lax.broadcasted_iota(jnp.int32, (1, LANES), 1)
    head0 = lane < HEAD_DIM
    li = lax.broadcasted_iota(jnp.int32, (LANES, LANES), 0)
    lj = lax.broadcasted_iota(jnp.int32, (LANES, LANES), 1)
    seg_ones = ((li // HEAD_DIM) == (lj // HEAD_DIM)).astype(BF16)
    ti = lax.broadcasted_iota(jnp.int32, (tb, tb), 0)
    tj = lax.broadcasted_iota(jnp.int32, (tb, tb), 1)
    cum_ones = ((ti // C == tj // C) & (tj <= ti)).astype(BF16)
    si = lax.broadcasted_iota(jnp.int32, (S2, S2), 0)
    sj = lax.broadcasted_iota(jnp.int32, (S2, S2), 1)
    same = (si // C) == (sj // C)
    strict = same & (sj < si)
    incl = same & (sj <= si)
    eye = si == sj
    level_masks = []
    bsz = 1
    while bsz < C:
        level_masks.append(((si // (2 * bsz)) == (sj // (2 * bsz)))
                           & (((si // bsz) % 2) == 1) & (((sj // bsz) % 2) == 0))
        bsz *= 2

    def seg_sum(x):
        return _mm(x, seg_ones, na=2)

    def stack(x):
        return jnp.concatenate([jnp.where(head0, x, 0.0), jnp.where(head0, 0.0, x)], axis=0)

    r, v, k2, e_in, a_t, r_t, b_t, k_t = ([] for _ in range(8))
    for p in range(n_slab):
        cols = slice(p * LANES, (p + 1) * LANES)
        ld = ld_ref[:, cols]
        ai = ai_ref[:, cols]
        k = k_ref[:, cols]
        kk = k * kk_ref[:, cols]
        kk = kk / jnp.maximum(jnp.sqrt(seg_sum(kk * kk)), 1e-12)
        cum = _mm(cum_ones, ld, nb=2)
        e_neg = jnp.exp(-cum)
        r.append(r_ref[:, cols])
        v.append(v_ref[:, cols])
        k2.append(k * (1.0 + (ai - 1.0) * ka_ref[:, cols]))
        e_in.append(jnp.exp(cum))
        a_t.append(-kk * jnp.exp(cum - ld))
        r_t.append(r[p] * e_in[p])
        b_t.append(kk * ai * e_neg)
        k_t.append(k2[p] * e_neg)

    rows = lambda c: slice(c * C, (c + 1) * C)
    p_end = [e_in[p][(c + 1) * C - 1:(c + 1) * C, :] for p, c in units]
    a_s = [stack(a_t[p][rows(c)]).astype(BF16) for p, c in units]
    r_s = [stack(r_t[p][rows(c)]) for p, c in units]
    v_s = [stack(v[p][rows(c)]).astype(BF16) for p, c in units]
    bk = [jnp.concatenate([stack(b_t[p][rows(c)]), stack(k_t[p][rows(c)])], axis=0) for p, c in units]
    n = range(len(units))

    sc = [_dg(jnp.concatenate([a_s[u], r_s[u].astype(BF16)], axis=0), bk[u].astype(BF16), _NT) for u in n]
    a_ab = [jnp.where(strict, sc[u][:S2, :S2], 0.0) for u in n]
    a_ak = [jnp.where(strict, sc[u][:S2, S2:], 0.0).astype(BF16) for u in n]
    a_rb = [jnp.where(incl, sc[u][S2:, :S2], 0.0).astype(BF16) for u in n]
    a_rk = [jnp.where(incl, sc[u][S2:, S2:], 0.0).astype(BF16) for u in n]

    def odd_rows(x, b):
        return jnp.concatenate([x[i * b:(i + 1) * b] for i in range(1, S2 // b, 2)], axis=0)

    def add_to_odd_rows(x, upd, b):
        return jnp.concatenate([x[i * b:(i + 1) * b] + upd[(i // 2) * b:(i // 2 + 1) * b] if i % 2 else
                                x[i * b:(i + 1) * b] for i in range(S2 // b)], axis=0)

    pinv = [jnp.where(eye, 1.0, jnp.where(level_masks[0], a_ab[u], 0.0)) for u in n]
    for lvl, mask in enumerate(level_masks[1:], start=1):
        b = 1 << lvl
        pb = [pinv[u].astype(BF16) for u in n]
        lower = [jnp.where(mask, a_ab[u], 0.0).astype(BF16) for u in n]
        if b % 16 == 0:
            xl = [_dg(odd_rows(pb[u], b), lower[u]) for u in n]
            pinv = [add_to_odd_rows(pinv[u], _dg(xl[u].astype(BF16), pb[u]), b) for u in n]
        else:
            xl = [_dg(pb[u], lower[u]) for u in n]
            pinv = [pinv[u] + _dg(xl[u].astype(BF16), pb[u]) for u in n]

    av = [_dg(a_ak[u], v_s[u]) for u in n]
    ta = [_dg(pinv[u].astype(BF16), jnp.concatenate([a_s[u], av[u].astype(BF16)], axis=1)).astype(BF16)
          for u in n]
    zeros = jnp.zeros((S2, LANES), BF16)
    rhs = [jnp.concatenate([ta[u], jnp.concatenate([zeros, v_s[u]], axis=1)], axis=0) for u in n]
    out_side = [_dg(jnp.concatenate([a_rb[u], a_rk[u]], axis=1), rhs[u]) for u in n]
    state_side = [_dg((bk[u] * p_end[u]).astype(BF16), rhs[u], _TN) for u in n]
    r_h = [(r_s[u] + out_side[u][:, :LANES]).astype(BF16) for u in n]
    y_h = [out_side[u][:, LANES:] for u in n]
    m_mat = [(jnp.where(eye, p_end[u], 0.0) + state_side[u][:, :LANES]).astype(BF16) for u in n]
    g_mat = [state_side[u][:, LANES:] for u in n]

    state = [state_ref[p] for p in range(n_slab)]
    ys = [[] for _ in range(n_slab)]
    for u, (p, c) in enumerate(units):
        sb = state[p].astype(BF16)
        y_st = _dg(r_h[u], sb) + y_h[u]
        ys[p].append(y_st[:C] + y_st[C:])
        state[p] = _dg(m_mat[u], sb) + g_mat[u]

    inv_n = 1.0 / HEAD_DIM
    for p in range(n_slab):
        cols = slice(p * LANES, (p + 1) * LANES)
        state_ref[p] = state[p]
        y = jnp.concatenate(ys[p], axis=0) if n_chunks > 1 else ys[p][0]
        mean = seg_sum(y) * inv_n
        yc = y - mean
        var = seg_sum(yc * yc) * inv_n
        yn = yc * lax.rsqrt(var + LNX_EPS) * lw_ref[:, cols] + lb_ref[:, cols]
        bonus = seg_sum(r[p] * k2[p] * rk_ref[:, cols]) * v[p]
        y_ref[:, cols] = (yn + bonus) * g_ref[:, cols]


def _rwkv_call(rkv, ld, ai, g, k_k, k_a, r_k, lnx_w, lnx_b, *, batch, seq, tb):
    M, W = ld.shape
    steps = seq // tb
    tok = lambda off: pl.BlockSpec((tb, W), lambda b, s: (b * steps + s, off))
    par = pl.BlockSpec((1, W), lambda b, s: (0, 0))
    return pl.pallas_call(
        functools.partial(_rwkv_kernel, chunk=RWKV_CHUNK),
        grid=(batch, steps),
        in_specs=[tok(0), tok(1), tok(2), tok(0), tok(0), tok(0), par, par, par, par, par],
        out_specs=tok(0),
        out_shape=jax.ShapeDtypeStruct((M, W), F32),
        scratch_shapes=[pltpu.VMEM((W // LANES, LANES, LANES), F32)],
        compiler_params=_cparams(("arbitrary", "arbitrary")),
        name="rwkv",
    )(rkv, rkv, rkv, ld, ai, g, k_k.reshape(1, W), k_a.reshape(1, W), r_k.reshape(1, W),
      lnx_w.reshape(1, W), lnx_b.reshape(1, W))


def _stick_kernel(qt_ref, k_ref, vt_ref, o_ref, acc_ref, run_ref, *, tile, group):
    first = pl.program_id(2) * group
    si = lax.broadcasted_iota(jnp.int32, (tile, 2 * tile), 0)
    ti = lax.broadcasted_iota(jnp.int32, (tile, 2 * tile), 1) % tile
    past = si < ti
    ui = lax.broadcasted_iota(jnp.int32, (tile, tile), 0)
    uj = lax.broadcasted_iota(jnp.int32, (tile, tile), 1)
    later_ones = (uj > ui).astype(BF16)
    feat0 = lax.broadcasted_iota(jnp.int32, (LANES, tile), 0) < HEAD_DIM
    zero = jnp.zeros((LANES, tile), BF16)
    q_pair = [jnp.concatenate([jnp.where(feat0, qt_ref[s], zero), jnp.where(feat0, zero, qt_ref[s])], axis=1)
              for s in range(group)]
    n = range(group)

    def visit(d, diagonal):
        kb, vt = [], []
        for s in n:
            jb = first + s - d
            blk = jnp.maximum(jb, 0)
            kb.append(k_ref[pl.ds(pl.multiple_of(blk * tile, tile), tile), :])
            vt.append(vt_ref[blk] if diagonal else jnp.where(jb >= 0, vt_ref[blk], zero))
        z = [_dg(kb[s], q_pair[s]) for s in n]
        ls = [_log_sigmoid(x) for x in z]
        lk = [ls[s] - z[s] for s in n]
        if diagonal:
            lk = [jnp.where(past, x, 0.0) for x in lk]
        later = [_dg(later_ones, lk[s].astype(BF16)) for s in n]
        worst = None
        for s in n:
            total = later[s][0:1, :] + lk[s][0:1, :]
            if diagonal:
                w = jnp.where(past, jnp.exp(ls[s] + later[s]), 0.0)
                run = total
            else:
                run = run_ref[s]
                w = jnp.exp(ls[s] + later[s] + run)
                run = run + total
            pv = _dg(vt[s], w.astype(BF16))
            pv = jnp.concatenate([pv[:HEAD_DIM, :tile], pv[HEAD_DIM:, tile:]], axis=0)
            acc_ref[s] = pv if diagonal else acc_ref[s] + pv
            run_ref[s] = run
            worst = run if worst is None else jnp.maximum(worst, run)
        return worst

    worst = visit(0, True)
    for d in range(1, SB_FIXED_VISITS):
        worst = visit(d, False)

    def cond(carry):
        d, worst = carry
        return (d < first + group) & (worst > -EXP_UNDERFLOW)

    def body(carry):
        d, _ = carry
        for j in range(SB_VISITS_PER_TEST):
            worst = visit(d + j, False)
        return d + SB_VISITS_PER_TEST, jnp.max(worst)

    lax.while_loop(cond, body, (jnp.int32(SB_FIXED_VISITS), jnp.max(worst)))
    for s in n:
        o_ref[s * tile:(s + 1) * tile, :] = acc_ref[s].T.astype(o_ref.dtype)


def _stick_call(ksb, qt, vt, *, batch, seq, group):
    M, W = ksb.shape
    n_slab = W // LANES
    tile = SB_TILE
    rows = tile * group
    steps = seq // rows
    blocks = seq // tile
    return pl.pallas_call(
        functools.partial(_stick_kernel, tile=tile, group=group),
        grid=(batch, n_slab, steps),
        in_specs=[pl.BlockSpec((group, LANES, tile), lambda b, p, i: (b * steps + i, p, 0)),
                  pl.BlockSpec((seq, LANES), lambda b, p, i: (b, p)),
                  pl.BlockSpec((blocks, LANES, tile), lambda b, p, i: (b, p, 0))],
        out_specs=pl.BlockSpec((rows, LANES), lambda b, p, i: (b * steps + i, p)),
        out_shape=jax.ShapeDtypeStruct((M, W), BF16),
        scratch_shapes=[pltpu.VMEM((group, LANES, tile), F32), pltpu.VMEM((group, 1, 2 * tile), F32)],
        compiler_params=_cparams(("arbitrary", "arbitrary", "arbitrary")),
        name="stick",
    )(qt, ksb, vt)


def _post_kernel(x_ref, ya_ref, yb_ref, gate_ref, mod_ref, wa_ref, wb_ref, wo_ref, g2_ref, gf_ref,
                 wg_ref, wu_ref, wd_ref, o_ref, *, final_norm):
    D = x_ref.shape[1]
    gt1, sh2, sc2, gt2 = (mod_ref[0, j:j + 1, :] for j in (2, 3, 4, 5))
    pa = _dg(ya_ref[...].astype(BF16), wa_ref[...])
    pb = _dg(yb_ref[...], wb_ref[...])
    merged = gate_ref[:, :D].astype(F32) * pa + gate_ref[:, D:].astype(F32) * pb
    x1 = x_ref[...] + gt1 * _dg(merged.astype(BF16), wo_ref[...])

    ms = jnp.mean(x1 * x1, axis=-1, keepdims=True)
    hb = ((x1 * lax.rsqrt(ms + NORM_EPS) * g2_ref[...]) * (1.0 + sc2) + sh2).astype(BF16)
    ug = _dg(hb, wg_ref[...])
    uu = _dg(hb, wu_ref[...])
    act = (ug * _sigmoid(ug) * uu).astype(BF16)
    x2 = x1 + gt2 * _dg(act, wd_ref[...])
    if final_norm:
        ms = jnp.mean(x2 * x2, axis=-1, keepdims=True)
        x2 = x2 * lax.rsqrt(ms + NORM_EPS) * gf_ref[...]
    o_ref[...] = x2


def _post_call(x2, ya, yb, gates, mod3, w_pa, w_pb, w_out, norm2_g, final_g, w_g, w_u, w_d,
               *, seq, tm, final_norm):
    M, D = x2.shape
    tiles_per_seq = seq // tm
    const = _const_spec
    rows = lambda n: pl.BlockSpec((tm, n), lambda i: (i, 0))
    return pl.pallas_call(
        functools.partial(_post_kernel, final_norm=final_norm),
        grid=(M // tm,),
        in_specs=[rows(D), rows(ya.shape[1]), rows(yb.shape[1]), rows(gates.shape[1]),
                  pl.BlockSpec((1, mod3.shape[1], D), lambda i: (i // tiles_per_seq, 0, 0)),
                  const(w_pa.shape), const(w_pb.shape), const(w_out.shape), const((1, D)), const((1, D)),
                  const(w_g.shape), const(w_u.shape), const(w_d.shape)],
        out_specs=rows(D),
        out_shape=jax.ShapeDtypeStruct((M, D), F32),
        compiler_params=_cparams(("arbitrary",)),
        name="post",
    )(x2, ya, yb, gates, mod3, w_pa, w_pb, w_out, norm2_g.reshape(1, D), final_g.reshape(1, D), w_g, w_u, w_d)


def _tiles(seq):
    pick = lambda want: max(t for t in (8, 16, 32, 64, 128, 256, 512, 1024) if t <= want and seq % t == 0)
    return dict(inproj=pick(512), rwkv=pick(256), stick=pick(SB_GROUP * SB_TILE), post=pick(512))


def kernel(x, c, w_ada, b_ada, norm1_g, w_in, mu_rkv, mu_wag, w0, w1, w2, a0, a1, a2, g1, g2, k_k, k_a, r_k,
           lnx_w, lnx_b, w_proj_a, w_proj_b, w_out, norm2_g, w_ffn_in, w_ffn_out, final_g):
    B, T, D = x.shape
    depth = w_ada.shape[0]
    W = w0.shape[1]
    S = w_proj_b.shape[1]
    F = w_ffn_out.shape[1]
    assert T % SB_TILE == 0 and W % LANES == 0 and S % LANES == 0
    tiles = _tiles(T)
    x2 = x.reshape(B * T, D)
    for l in range(depth):
        mod3 = _mod_call(c, w_ada[l], b_ada[l]).reshape(B, 6, D)
        wl = w_in[l].astype(BF16)
        w_q, w_k, w_v = (wl[:, 3 * W + j * S:3 * W + (j + 1) * S] for j in range(3))
        rkv, ld, ai, g, ksb, qt, vt, gates = _inproj_call(
            x2, mod3, norm1_g[l], wl[:, :3 * W], w_k, jnp.concatenate([w_q, w_v], axis=1).T,
            wl[:, 3 * W + 3 * S:],
            mu_rkv[l], mu_wag[l], w1[l].astype(BF16), w2[l].astype(BF16), w0[l],
            a1[l].astype(BF16), a2[l].astype(BF16), a0[l], g1[l].astype(BF16), g2[l].astype(BF16),
            seq=T, tm=tiles["inproj"])
        ya = _rwkv_call(rkv, ld, ai, g, k_k[l], k_a[l], r_k[l], lnx_w[l], lnx_b[l],
                        batch=B, seq=T, tb=tiles["rwkv"])
        yb = _stick_call(ksb, qt, vt, batch=B, seq=T, group=tiles["stick"] // SB_TILE)
        wf = w_ffn_in[l].astype(BF16)
        x2 = _post_call(x2, ya, yb, gates, mod3, w_proj_a[l].astype(BF16), w_proj_b[l].astype(BF16),
                        w_out[l].astype(BF16), norm2_g[l], final_g, wf[:, :F], wf[:, F:],
                        w_ffn_out[l].astype(BF16), seq=T, tm=tiles["post"], final_norm=(l == depth - 1))
    return x2.reshape(B, T, D)
```

```python
import functools

import jax
import jax.numpy as jnp
from jax import lax
from jax.experimental import pallas as pl
from jax.experimental.pallas import tpu as pltpu

F32 = jnp.float32
BF16 = jnp.bfloat16

HEAD_DIM = 64
LANES = 128
NORM_EPS = 1e-6
LNX_EPS = 64e-5
RWKV_CHUNK = 64
SB_TILE = 128
SB_GROUP = 8
SB_FIXED_VISITS = 4
SB_VISITS_PER_TEST = 2
EXP_UNDERFLOW = 104.0
VMEM_LIMIT = 56 * 1024 * 1024


def _cparams(sem):
    return pltpu.CompilerParams(dimension_semantics=sem, vmem_limit_bytes=VMEM_LIMIT)


def _const_spec(shape):
    return pl.BlockSpec(shape, lambda i: (0,) * len(shape), pipeline_mode=pl.Buffered(1))


_NN = (((1,), (0,)), ((), ()))
_NT = (((1,), (1,)), ((), ()))
_TN = (((0,), (0,)), ((), ()))


def _dg(a, b, dn=_NN):
    return lax.dot_general(a, b, dn, preferred_element_type=F32)


def _split(x, n):
    pieces, rem = [], x
    for i in range(n):
        p = rem.astype(BF16)
        pieces.append(p)
        if i + 1 < n:
            rem = rem - p.astype(F32)
    return pieces


def _mm(a, b, dn=_NN, na=1, nb=1):
    pa = _split(a, na) if a.dtype != BF16 else [a]
    pb = _split(b, nb) if b.dtype != BF16 else [b]
    order = max(len(pa), len(pb))
    acc = None
    for i, x in enumerate(pa):
        for j, y in enumerate(pb):
            if i + j < order:
                t = _dg(x, y, dn)
                acc = t if acc is None else acc + t
    return acc


def _sigmoid(x):
    return 1.0 / (1.0 + jnp.exp(-x))


def _log_sigmoid(x):
    return jnp.minimum(x, 0.0) - jnp.log(1.0 + jnp.exp(-jnp.abs(x)))


def _mod_kernel(c_ref, w_ref, b_ref, o_ref):
    c = c_ref[...]
    c_act = c * _sigmoid(c)
    o_ref[...] = _mm(c_act, w_ref[...], na=3, nb=3) + b_ref[...]


def _mod_call(c, w_ada, b_ada):
    B, D = c.shape
    n = w_ada.shape[1]
    return pl.pallas_call(
        _mod_kernel,
        grid=(n // D,),
        in_specs=[pl.BlockSpec((B, D), lambda j: (0, 0)),
                  pl.BlockSpec((D, D), lambda j: (0, j)),
                  pl.BlockSpec((1, D), lambda j: (0, j))],
        out_specs=pl.BlockSpec((B, D), lambda j: (0, j)),
        out_shape=jax.ShapeDtypeStruct((B, n), F32),
        compiler_params=_cparams(("arbitrary",)),
        name="mod",
    )(c, w_ada, b_ada.reshape(1, n))


def _shift_rows(cur, prev_row):
    rolled = pltpu.roll(cur, 1, 0)
    row = lax.broadcasted_iota(jnp.int32, cur.shape, 0)
    return jnp.where(row == 0, prev_row, rolled)


def _inproj_kernel(x_ref, xp_ref, mod_ref, g_ref, wrkv_ref, wk_ref, wqvt_ref, wgate_ref, murkv_ref, muwag_ref,
                   w1_ref, w2_ref, w0_ref, a1_ref, a2_ref, a0_ref, g1_ref, g2_ref,
                   rkv_ref, ld_ref, ai_ref, go_ref, ksb_ref, qt_ref, vt_ref, gate_ref, *, tiles_per_seq):
    first = (pl.program_id(0) % tiles_per_seq) == 0
    shift = mod_ref[0, 0:1, :]
    scale = mod_ref[0, 1:2, :]
    gain = g_ref[...]

    def norm_mod(xv):
        ms = jnp.mean(xv * xv, axis=-1, keepdims=True)
        return (xv * lax.rsqrt(ms + NORM_EPS) * gain) * (1.0 + scale) + shift

    hp = jnp.where(first, 0.0, norm_mod(xp_ref[...]))
    h = norm_mod(x_ref[...])
    hb = h.astype(BF16)

    gate_ref[...] = _sigmoid(_dg(hb, wgate_ref[...])).astype(BF16)

    xx = _shift_rows(h, hp[7:8, :]) - h
    xw = (h + xx * muwag_ref[0:1, :]).astype(BF16)
    xa = (h + xx * muwag_ref[1:2, :]).astype(BF16)
    xg = (h + xx * muwag_ref[2:3, :]).astype(BF16)
    lw = _dg(xw, w1_ref[...])
    la = _dg(xa, a1_ref[...])
    lg = _dg(xg, g1_ref[...])

    p = _dg(hb, wrkv_ref[...])
    p_prev = _dg(hp.astype(BF16), wrkv_ref[...])[7:8, :]
    rkv_ref[...] = p + (_shift_rows(p, p_prev) - p) * murkv_ref[...]

    u = w0_ref[...] + _dg(jnp.tanh(lw).astype(BF16), w2_ref[...])
    w_log = _log_sigmoid(u) - 0.5
    ld_ref[...] = -jnp.exp(w_log)
    ai_ref[...] = _sigmoid(a0_ref[...] + _dg(la.astype(BF16), a2_ref[...]))
    go_ref[...] = _dg(_sigmoid(lg).astype(BF16), g2_ref[...])

    ksb_ref[...] = _dg(hb, wk_ref[...]).astype(BF16)
    qv = _dg(wqvt_ref[...], hb, _NT)
    n_feat = qt_ref.shape[1]
    for blk in range(x_ref.shape[0] // SB_TILE):
        cols = slice(blk * SB_TILE, (blk + 1) * SB_TILE)
        qt_ref[blk] = (qv[:n_feat, cols] * (HEAD_DIM ** -0.5)).astype(BF16)
        vt_ref[blk] = qv[n_feat:, cols].astype(BF16)


def _inproj_call(x2, mod3, norm_g, w_rkv, w_k, w_qvt, w_gate, mu_rkv, mu_wag, w1, w2, w0, a1, a2, a0, g1, g2,
                 *, seq, tm):
    M, D = x2.shape
    W3 = w_rkv.shape[1]
    W = W3 // 3
    S = w_k.shape[1]
    tiles_per_seq = seq // tm
    const = _const_spec
    rows = lambda n: pl.BlockSpec((tm, n), lambda i: (i, 0))
    feat_major = pl.BlockSpec((tm // SB_TILE, S, SB_TILE), lambda i: (i, 0, 0))
    return pl.pallas_call(
        functools.partial(_inproj_kernel, tiles_per_seq=tiles_per_seq),
        grid=(M // tm,),
        in_specs=[rows(D),
                  pl.BlockSpec((8, D), lambda i: (jnp.maximum(i * (tm // 8) - 1, 0), 0)),
                  pl.BlockSpec((1, mod3.shape[1], D), lambda i: (i // tiles_per_seq, 0, 0)),
                  const((1, D)), const(w_rkv.shape), const(w_k.shape), const(w_qvt.shape), const(w_gate.shape),
                  const((1, W3)), const(mu_wag.shape),
                  const(w1.shape), const(w2.shape), const((1, W)),
                  const(a1.shape), const(a2.shape), const((1, W)),
                  const(g1.shape), const(g2.shape)],
        out_specs=[rows(W3), rows(W), rows(W), rows(W), rows(S), feat_major, feat_major, rows(w_gate.shape[1])],
        out_shape=[jax.ShapeDtypeStruct((M, W3), F32),
                   jax.ShapeDtypeStruct((M, W), F32),
                   jax.ShapeDtypeStruct((M, W), F32),
                   jax.ShapeDtypeStruct((M, W), F32),
                   jax.ShapeDtypeStruct((M, S), BF16),
                   jax.ShapeDtypeStruct((M // SB_TILE, S, SB_TILE), BF16),
                   jax.ShapeDtypeStruct((M // SB_TILE, S, SB_TILE), BF16),
                   jax.ShapeDtypeStruct((M, w_gate.shape[1]), BF16)],
        compiler_params=_cparams(("arbitrary",)),
        name="inproj",
    )(x2, x2, mod3, norm_g.reshape(1, D), w_rkv, w_k, w_qvt, w_gate, mu_rkv.reshape(1, W3), mu_wag,
      w1, w2, w0.reshape(1, W), a1, a2, a0.reshape(1, W), g1, g2)


def _rwkv_kernel(r_ref, k_ref, v_ref, ld_ref, ai_ref, g_ref, kk_ref, ka_ref, rk_ref, lw_ref, lb_ref,
                 y_ref, state_ref, *, chunk):
    C = chunk
    tb, W = r_ref.shape
    n_slab = W // LANES
    n_chunks = tb // C
    S2 = 2 * C

    @pl.when(pl.program_id(1) == 0)
    def _():
        state_ref[...] = jnp.zeros_like(state_ref)

    lane = lax.broadcasted_iota(jnp.int32, (1, LANES), 1)
    head0 = lane < HEAD_DIM
    li = lax.broadcasted_iota(jnp.int32, (LANES, LANES), 0)
    lj = lax.broadcasted_iota(jnp.int32, (LANES, LANES), 1)
    seg_ones = ((li // HEAD_DIM) == (lj // HEAD_DIM)).astype(BF16)
    ti = lax.broadcasted_iota(jnp.int32, (tb, tb), 0)
    tj = lax.broadcasted_iota(jnp.int32, (tb, tb), 1)
    cum_ones = ((ti // C == tj // C) & (tj <= ti)).astype(BF16)
    si = lax.broadcasted_iota(jnp.int32, (S2, S2), 0)
    sj = lax.broadcasted_iota(jnp.int32, (S2, S2), 1)
    same = (si // C) == (sj // C)
    strict = same & (sj < si)
    incl = same & (sj <= si)
    eye = si == sj
    level_masks = []
    bsz = 1
    while bsz < C:
        level_masks.append(((si // (2 * bsz)) == (sj // (2 * bsz)))
                           & (((si // bsz) % 2) == 1) & (((sj // bsz) % 2) == 0))
        bsz *= 2

    def seg_sum(x):
        return _mm(x, seg_ones, na=2)

    def stack(x):
        return jnp.concatenate([jnp.where(head0, x, 0.0), jnp.where(head0, 0.0, x)], axis=0)

    zeros = jnp.zeros((S2, LANES), BF16)
    rows = lambda c: slice(c * C, (c + 1) * C)
    cols = lambda p: slice(p * LANES, (p + 1) * LANES)

    def odd_rows(x, b):
        return jnp.concatenate([x[i * b:(i + 1) * b] for i in range(1, S2 // b, 2)], axis=0)

    def add_to_odd_rows(x, upd, b):
        return jnp.concatenate([x[i * b:(i + 1) * b] + upd[(i // 2) * b:(i // 2 + 1) * b] if i % 2 else
                                x[i * b:(i + 1) * b] for i in range(S2 // b)], axis=0)

    def prepare(p):
        ld = ld_ref[:, cols(p)]
        ai = ai_ref[:, cols(p)]
        k = k_ref[:, cols(p)]
        r = r_ref[:, cols(p)]
        kk = k * kk_ref[:, cols(p)]
        kk = kk / jnp.maximum(jnp.sqrt(seg_sum(kk * kk)), 1e-12)
        cum = _mm(cum_ones, ld, nb=2)
        e_in = jnp.exp(cum)
        e_neg = jnp.exp(-cum)
        k2 = k * (1.0 + (ai - 1.0) * ka_ref[:, cols(p)])
        return dict(r=r, v=v_ref[:, cols(p)], k2=k2, e_in=e_in, a_t=-kk * jnp.exp(cum - ld), r_t=r * e_in,
                    b_t=kk * ai * e_neg, k_t=k2 * e_neg)

    def chunk_products(tok, units, out):
        n = range(len(units))
        p_end = [tok[p]["e_in"][(c + 1) * C - 1:(c + 1) * C, :] for p, c in units]
        a_s = [stack(tok[p]["a_t"][rows(c)]).astype(BF16) for p, c in units]
        r_s = [stack(tok[p]["r_t"][rows(c)]) for p, c in units]
        v_s = [stack(tok[p]["v"][rows(c)]).astype(BF16) for p, c in units]
        bk = [jnp.concatenate([stack(tok[p]["b_t"][rows(c)]), stack(tok[p]["k_t"][rows(c)])], axis=0)
              for p, c in units]
        sc = [_dg(jnp.concatenate([a_s[u], r_s[u].astype(BF16)], axis=0), bk[u].astype(BF16), _NT) for u in n]
        yield
        a_ab = [jnp.where(strict, sc[u][:S2, :S2], 0.0) for u in n]
        a_ak = [jnp.where(strict, sc[u][:S2, S2:], 0.0).astype(BF16) for u in n]
        a_rb = [jnp.where(incl, sc[u][S2:, :S2], 0.0).astype(BF16) for u in n]
        a_rk = [jnp.where(incl, sc[u][S2:, S2:], 0.0).astype(BF16) for u in n]
        pinv = [jnp.where(eye, 1.0, jnp.where(level_masks[0], a_ab[u], 0.0)) for u in n]
        for lvl, mask in enumerate(level_masks[1:], start=1):
            b = 1 << lvl
            pb = [pinv[u].astype(BF16) for u in n]
            lower = [jnp.where(mask, a_ab[u], 0.0).astype(BF16) for u in n]
            if b % 16 == 0:
                xl = [_dg(odd_rows(pb[u], b), lower[u]) for u in n]
                yield
                pinv = [add_to_odd_rows(pinv[u], _dg(xl[u].astype(BF16), pb[u]), b) for u in n]
            else:
                xl = [_dg(pb[u], lower[u]) for u in n]
                yield
                pinv = [pinv[u] + _dg(xl[u].astype(BF16), pb[u]) for u in n]
            yield
        av = [_dg(a_ak[u], v_s[u]) for u in n]
        yield
        ta = [_dg(pinv[u].astype(BF16), jnp.concatenate([a_s[u], av[u].astype(BF16)], axis=1)).astype(BF16)
              for u in n]
        yield
        rhs = [jnp.concatenate([ta[u], jnp.concatenate([zeros, v_s[u]], axis=1)], axis=0) for u in n]
        out_side = [_dg(jnp.concatenate([a_rb[u], a_rk[u]], axis=1), rhs[u]) for u in n]
        state_side = [_dg((bk[u] * p_end[u]).astype(BF16), rhs[u], _TN) for u in n]
        for u, key in enumerate(units):
            out[key] = ((r_s[u] + out_side[u][:, :LANES]).astype(BF16),
                        out_side[u][:, LANES:],
                        (jnp.where(eye, p_end[u], 0.0) + state_side[u][:, :LANES]).astype(BF16),
                        state_side[u][:, LANES:])

    def finish(tok, slabs, prod):
        state = {p: state_ref[p] for p in slabs}
        ys = {p: [] for p in slabs}
        for c in range(n_chunks):
            for p in slabs:
                r_h, y_h, m_mat, g_mat = prod[(p, c)]
                sb = state[p].astype(BF16)
                y_st = _dg(r_h, sb) + y_h
                ys[p].append(y_st[:C] + y_st[C:])
                state[p] = _dg(m_mat, sb) + g_mat
            yield
        inv_n = 1.0 / HEAD_DIM
        for p in slabs:
            state_ref[p] = state[p]
            y = jnp.concatenate(ys[p], axis=0) if n_chunks > 1 else ys[p][0]
            mean = seg_sum(y) * inv_n
            yc = y - mean
            var = seg_sum(yc * yc) * inv_n
            yn = yc * lax.rsqrt(var + LNX_EPS) * lw_ref[:, cols(p)] + lb_ref[:, cols(p)]
            bonus = seg_sum(tok[p]["r"] * tok[p]["k2"] * rk_ref[:, cols(p)]) * tok[p]["v"]
            y_ref[:, cols(p)] = (yn + bonus) * g_ref[:, cols(p)]
            yield

    def chain(gens):
        for g in gens:
            yield from g

    waves = [list(range(n_slab))]
    tok, prod = {}, {}

    def prepare_wave(slabs):
        for p in slabs:
            tok[p] = prepare(p)
            yield

    side = prepare_wave(waves[0])
    for i, slabs in enumerate(waves):
        for _ in side:
            pass
        side = chain(([finish(tok, waves[i - 1], prod)] if i > 0 else [])
                     + ([prepare_wave(waves[i + 1])] if i + 1 < len(waves) else []))
        for _ in chunk_products(tok, [(p, c) for c in range(n_chunks) for p in slabs], prod):
            next(side, None)
    for _ in chain([side, finish(tok, waves[-1], prod)]):
        pass


def _rwkv_call(rkv, ld, ai, g, k_k, k_a, r_k, lnx_w, lnx_b, *, batch, seq, tb):
    M, W = ld.shape
    steps = seq // tb
    tok = lambda off: pl.BlockSpec((tb, W), lambda b, s: (b * steps + s, off))
    par = pl.BlockSpec((1, W), lambda b, s: (0, 0))
    return pl.pallas_call(
        functools.partial(_rwkv_kernel, chunk=RWKV_CHUNK),
        grid=(batch, steps),
        in_specs=[tok(0), tok(1), tok(2), tok(0), tok(0), tok(0), par, par, par, par, par],
        out_specs=tok(0),
        out_shape=jax.ShapeDtypeStruct((M, W), F32),
        scratch_shapes=[pltpu.VMEM((W // LANES, LANES, LANES), F32)],
        compiler_params=_cparams(("arbitrary", "arbitrary")),
        name="rwkv",
    )(rkv, rkv, rkv, ld, ai, g, k_k.reshape(1, W), k_a.reshape(1, W), r_k.reshape(1, W),
      lnx_w.reshape(1, W), lnx_b.reshape(1, W))


def _stick_kernel(qt_ref, k_ref, vt_ref, o_ref, acc_ref, run_ref, *, tile, group):
    first = pl.program_id(2) * group
    si = lax.broadcasted_iota(jnp.int32, (tile, 2 * tile), 0)
    ti = lax.broadcasted_iota(jnp.int32, (tile, 2 * tile), 1) % tile
    past = si < ti
    ui = lax.broadcasted_iota(jnp.int32, (tile, tile), 0)
    uj = lax.broadcasted_iota(jnp.int32, (tile, tile), 1)
    later_ones = (uj > ui).astype(BF16)
    feat0 = lax.broadcasted_iota(jnp.int32, (LANES, tile), 0) < HEAD_DIM
    zero = jnp.zeros((LANES, tile), BF16)
    q_pair = [jnp.concatenate([jnp.where(feat0, qt_ref[s], zero), jnp.where(feat0, zero, qt_ref[s])], axis=1)
              for s in range(group)]
    n = range(group)

    def visit(d, diagonal):
        kb, vt = [], []
        for s in n:
            jb = first + s - d
            blk = jnp.maximum(jb, 0)
            kb.append(k_ref[pl.ds(pl.multiple_of(blk * tile, tile), tile), :])
            vt.append(vt_ref[blk] if diagonal else jnp.where(jb >= 0, vt_ref[blk], zero))
        z = [_dg(kb[s], q_pair[s]) for s in n]
        ls = [_log_sigmoid(x) for x in z]
        lk = [ls[s] - z[s] for s in n]
        if diagonal:
            lk = [jnp.where(past, x, 0.0) for x in lk]
        later = [_dg(later_ones, lk[s].astype(BF16)) for s in n]
        worst = None
        for s in n:
            total = later[s][0:1, :] + lk[s][0:1, :]
            if diagonal:
                w = jnp.where(past, jnp.exp(ls[s] + later[s]), 0.0)
                run = total
            else:
                run = run_ref[s]
                w = jnp.exp(ls[s] + later[s] + run)
                run = run + total
            pv = _dg(vt[s], w.astype(BF16))
            pv = jnp.concatenate([pv[:HEAD_DIM, :tile], pv[HEAD_DIM:, tile:]], axis=0)
            acc_ref[s] = pv if diagonal else acc_ref[s] + pv
            run_ref[s] = run
            worst = run if worst is None else jnp.maximum(worst, run)
        return worst

    worst = visit(0, True)
    for d in range(1, SB_FIXED_VISITS):
        worst = visit(d, False)

    def cond(carry):
        d, worst = carry
        return (d < first + group) & (worst > -EXP_UNDERFLOW)

    def body(carry):
        d, _ = carry
        for j in range(SB_VISITS_PER_TEST):
            worst = visit(d + j, False)
        return d + SB_VISITS_PER_TEST, jnp.max(worst)

    lax.while_loop(cond, body, (jnp.int32(SB_FIXED_VISITS), jnp.max(worst)))
    for s in n:
        o_ref[s * tile:(s + 1) * tile, :] = acc_ref[s].T.astype(o_ref.dtype)


def _stick_call(ksb, qt, vt, *, batch, seq, group):
    M, W = ksb.shape
    n_slab = W // LANES
    tile = SB_TILE
    rows = tile * group
    steps = seq // rows
    blocks = seq // tile
    return pl.pallas_call(
        functools.partial(_stick_kernel, tile=tile, group=group),
        grid=(batch, n_slab, steps),
        in_specs=[pl.BlockSpec((group, LANES, tile), lambda b, p, i: (b * steps + i, p, 0)),
                  pl.BlockSpec((seq, LANES), lambda b, p, i: (b, p)),
                  pl.BlockSpec((blocks, LANES, tile), lambda b, p, i: (b, p, 0))],
        out_specs=pl.BlockSpec((rows, LANES), lambda b, p, i: (b * steps + i, p)),
        out_shape=jax.ShapeDtypeStruct((M, W), BF16),
        scratch_shapes=[pltpu.VMEM((group, LANES, tile), F32), pltpu.VMEM((group, 1, 2 * tile), F32)],
        compiler_params=_cparams(("arbitrary", "arbitrary", "arbitrary")),
        name="stick",
    )(qt, ksb, vt)


def _post_kernel(x_ref, ya_ref, yb_ref, gate_ref, mod_ref, wa_ref, wb_ref, wo_ref, g2_ref, gf_ref,
                 wg_ref, wu_ref, wd_ref, o_ref, *, final_norm):
    D = x_ref.shape[1]
    gt1, sh2, sc2, gt2 = (mod_ref[0, j:j + 1, :] for j in (2, 3, 4, 5))
    pa = _dg(ya_ref[...].astype(BF16), wa_ref[...])
    pb = _dg(yb_ref[...], wb_ref[...])
    merged = gate_ref[:, :D].astype(F32) * pa + gate_ref[:, D:].astype(F32) * pb
    x1 = x_ref[...] + gt1 * _dg(merged.astype(BF16), wo_ref[...])

    ms = jnp.mean(x1 * x1, axis=-1, keepdims=True)
    hb = ((x1 * lax.rsqrt(ms + NORM_EPS) * g2_ref[...]) * (1.0 + sc2) + sh2).astype(BF16)
    ug = _dg(hb, wg_ref[...])
    uu = _dg(hb, wu_ref[...])
    act = (ug * _sigmoid(ug) * uu).astype(BF16)
    x2 = x1 + gt2 * _dg(act, wd_ref[...])
    if final_norm:
        ms = jnp.mean(x2 * x2, axis=-1, keepdims=True)
        x2 = x2 * lax.rsqrt(ms + NORM_EPS) * gf_ref[...]
    o_ref[...] = x2


def _post_call(x2, ya, yb, gates, mod3, w_pa, w_pb, w_out, norm2_g, final_g, w_g, w_u, w_d,
               *, seq, tm, final_norm):
    M, D = x2.shape
    tiles_per_seq = seq // tm
    const = _const_spec
    rows = lambda n: pl.BlockSpec((tm, n), lambda i: (i, 0))
    return pl.pallas_call(
        functools.partial(_post_kernel, final_norm=final_norm),
        grid=(M // tm,),
        in_specs=[rows(D), rows(ya.shape[1]), rows(yb.shape[1]), rows(gates.shape[1]),
                  pl.BlockSpec((1, mod3.shape[1], D), lambda i: (i // tiles_per_seq, 0, 0)),
                  const(w_pa.shape), const(w_pb.shape), const(w_out.shape), const((1, D)), const((1, D)),
                  const(w_g.shape), const(w_u.shape), const(w_d.shape)],
        out_specs=rows(D),
        out_shape=jax.ShapeDtypeStruct((M, D), F32),
        compiler_params=_cparams(("arbitrary",)),
        name="post",
    )(x2, ya, yb, gates, mod3, w_pa, w_pb, w_out, norm2_g.reshape(1, D), final_g.reshape(1, D), w_g, w_u, w_d)


def _tiles(seq):
    pick = lambda want: max(t for t in (8, 16, 32, 64, 128, 256, 512, 1024) if t <= want and seq % t == 0)
    return dict(inproj=pick(512), rwkv=pick(512), stick=pick(SB_GROUP * SB_TILE), post=pick(512))


def kernel(x, c, w_ada, b_ada, norm1_g, w_in, mu_rkv, mu_wag, w0, w1, w2, a0, a1, a2, g1, g2, k_k, k_a, r_k,
           lnx_w, lnx_b, w_proj_a, w_proj_b, w_out, norm2_g, w_ffn_in, w_ffn_out, final_g):
    B, T, D = x.shape
    depth = w_ada.shape[0]
    W = w0.shape[1]
    S = w_proj_b.shape[1]
    F = w_ffn_out.shape[1]
    assert T % SB_TILE == 0 and W % LANES == 0 and S % LANES == 0
    tiles = _tiles(T)
    x2 = x.reshape(B * T, D)
    for l in range(depth):
        mod3 = _mod_call(c, w_ada[l], b_ada[l]).reshape(B, 6, D)
        wl = w_in[l].astype(BF16)
        w_q, w_k, w_v = (wl[:, 3 * W + j * S:3 * W + (j + 1) * S] for j in range(3))
        rkv, ld, ai, g, ksb, qt, vt, gates = _inproj_call(
            x2, mod3, norm1_g[l], wl[:, :3 * W], w_k, jnp.concatenate([w_q, w_v], axis=1).T,
            wl[:, 3 * W + 3 * S:],
            mu_rkv[l], mu_wag[l], w1[l].astype(BF16), w2[l].astype(BF16), w0[l],
            a1[l].astype(BF16), a2[l].astype(BF16), a0[l], g1[l].astype(BF16), g2[l].astype(BF16),
            seq=T, tm=tiles["inproj"])
        ya = _rwkv_call(rkv, ld, ai, g, k_k[l], k_a[l], r_k[l], lnx_w[l], lnx_b[l],
                        batch=B, seq=T, tb=tiles["rwkv"])
        yb = _stick_call(ksb, qt, vt, batch=B, seq=T, group=tiles["stick"] // SB_TILE)
        wf = w_ffn_in[l].astype(BF16)
        x2 = _post_call(x2, ya, yb, gates, mod3, w_proj_a[l].astype(BF16), w_proj_b[l].astype(BF16),
                        w_out[l].astype(BF16), norm2_g[l], final_g, wf[:, :F], wf[:, F:],
                        w_ffn_out[l].astype(BF16), seq=T, tm=tiles["post"], final_norm=(l == depth - 1))
    return x2.reshape(B, T, D)
```

```python
import functools

import jax
import jax.numpy as jnp
from jax import lax
from jax.experimental import pallas as pl
from jax.experimental.pallas import tpu as pltpu

F32 = jnp.float32
BF16 = jnp.bfloat16

HEAD_DIM = 64
LANES = 128
NORM_EPS = 1e-6
LNX_EPS = 64e-5
RWKV_CHUNK = 64
SB_TILE = 128
SB_GROUP = 8
SB_FIXED_VISITS = 4
SB_VISITS_PER_TEST = 2
EXP_UNDERFLOW = 104.0
VMEM_LIMIT = 56 * 1024 * 1024


def _cparams(sem):
    return pltpu.CompilerParams(dimension_semantics=sem, vmem_limit_bytes=VMEM_LIMIT)


def _const_spec(shape):
    return pl.BlockSpec(shape, lambda i: (0,) * len(shape), pipeline_mode=pl.Buffered(1))


_NN = (((1,), (0,)), ((), ()))
_NT = (((1,), (1,)), ((), ()))
_TN = (((0,), (0,)), ((), ()))


def _dg(a, b, dn=_NN):
    return lax.dot_general(a, b, dn, preferred_element_type=F32)


def _split(x, n):
    pieces, rem = [], x
    for i in range(n):
        p = rem.astype(BF16)
        pieces.append(p)
        if i + 1 < n:
            rem = rem - p.astype(F32)
    return pieces


def _mm(a, b, dn=_NN, na=1, nb=1):
    pa = _split(a, na) if a.dtype != BF16 else [a]
    pb = _split(b, nb) if b.dtype != BF16 else [b]
    order = max(len(pa), len(pb))
    acc = None
    for i, x in enumerate(pa):
        for j, y in enumerate(pb):
            if i + j < order:
                t = _dg(x, y, dn)
                acc = t if acc is None else acc + t
    return acc


def _sigmoid(x):
    return 1.0 / (1.0 + jnp.exp(-x))


def _log_sigmoid(x):
    return jnp.minimum(x, 0.0) - jnp.log(1.0 + jnp.exp(-jnp.abs(x)))


def _mod_kernel(c_ref, w_ref, b_ref, o_ref):
    c = c_ref[...]
    c_act = c * _sigmoid(c)
    o_ref[...] = _mm(c_act, w_ref[...], na=2, nb=2) + b_ref[...]


def _mod_call(c, w_ada, b_ada):
    B, D = c.shape
    n = w_ada.shape[1]
    return pl.pallas_call(
        _mod_kernel,
        grid=(n // D,),
        in_specs=[pl.BlockSpec((B, D), lambda j: (0, 0)),
                  pl.BlockSpec((D, D), lambda j: (0, j)),
                  pl.BlockSpec((1, D), lambda j: (0, j))],
        out_specs=pl.BlockSpec((B, D), lambda j: (0, j)),
        out_shape=jax.ShapeDtypeStruct((B, n), F32),
        compiler_params=_cparams(("arbitrary",)),
        name="mod",
    )(c, w_ada, b_ada.reshape(1, n))


def _shift_rows(cur, prev_row):
    rolled = pltpu.roll(cur, 1, 0)
    row = lax.broadcasted_iota(jnp.int32, cur.shape, 0)
    return jnp.where(row == 0, prev_row, rolled)


def _inproj_kernel(x_ref, mod_ref, g_ref, wrkv_ref, wk_ref, wqvt_ref, wgate_ref, murkv_ref, muwag_ref,
                   w1_ref, w2_ref, w0_ref, a1_ref, a2_ref, a0_ref, g1_ref, g2_ref,
                   rkv_ref, ld_ref, ai_ref, go_ref, ksb_ref, qt_ref, vt_ref, gate_ref, hlast_ref, plast_ref,
                   *, tiles_per_seq):
    tm = x_ref.shape[0]

    @pl.when((pl.program_id(0) % tiles_per_seq) == 0)
    def _():
        hlast_ref[...] = jnp.zeros_like(hlast_ref)
        plast_ref[...] = jnp.zeros_like(plast_ref)

    shift = mod_ref[0, 0:1, :]
    scale = mod_ref[0, 1:2, :]
    ms = jnp.mean(x_ref[...] * x_ref[...], axis=-1, keepdims=True)
    h = (x_ref[...] * lax.rsqrt(ms + NORM_EPS) * g_ref[...]) * (1.0 + scale) + shift
    hb = h.astype(BF16)

    gate_ref[...] = _sigmoid(_dg(hb, wgate_ref[...])).astype(BF16)

    xx = _shift_rows(h, hlast_ref[...]) - h
    hlast_ref[...] = h[tm - 1:tm, :]
    xw = (h + xx * muwag_ref[0:1, :]).astype(BF16)
    xa = (h + xx * muwag_ref[1:2, :]).astype(BF16)
    xg = (h + xx * muwag_ref[2:3, :]).astype(BF16)
    lw = _dg(xw, w1_ref[...])
    la = _dg(xa, a1_ref[...])
    lg = _dg(xg, g1_ref[...])

    p = _dg(hb, wrkv_ref[...])
    rkv_ref[...] = p + (_shift_rows(p, plast_ref[...]) - p) * murkv_ref[...]
    plast_ref[...] = p[tm - 1:tm, :]

    u = w0_ref[...] + _dg(jnp.tanh(lw).astype(BF16), w2_ref[...])
    w_log = _log_sigmoid(u) - 0.5
    ld_ref[...] = -jnp.exp(w_log)
    ai_ref[...] = _sigmoid(a0_ref[...] + _dg(la.astype(BF16), a2_ref[...]))
    go_ref[...] = _dg(_sigmoid(lg).astype(BF16), g2_ref[...])

    ksb_ref[...] = _dg(hb, wk_ref[...]).astype(BF16)
    qv = _dg(wqvt_ref[...], hb, _NT)
    n_feat = qt_ref.shape[1]
    for blk in range(x_ref.shape[0] // SB_TILE):
        cols = slice(blk * SB_TILE, (blk + 1) * SB_TILE)
        qt_ref[blk] = (qv[:n_feat, cols] * (HEAD_DIM ** -0.5)).astype(BF16)
        vt_ref[blk] = qv[n_feat:, cols].astype(BF16)


def _inproj_call(x2, mod3, norm_g, w_rkv, w_k, w_qvt, w_gate, mu_rkv, mu_wag, w1, w2, w0, a1, a2, a0, g1, g2,
                 *, seq, tm):
    M, D = x2.shape
    W3 = w_rkv.shape[1]
    W = W3 // 3
    S = w_k.shape[1]
    tiles_per_seq = seq // tm
    const = _const_spec
    rows = lambda n: pl.BlockSpec((tm, n), lambda i: (i, 0))
    feat_major = pl.BlockSpec((tm // SB_TILE, S, SB_TILE), lambda i: (i, 0, 0))
    return pl.pallas_call(
        functools.partial(_inproj_kernel, tiles_per_seq=tiles_per_seq),
        grid=(M // tm,),
        in_specs=[rows(D),
                  pl.BlockSpec((1, mod3.shape[1], D), lambda i: (i // tiles_per_seq, 0, 0)),
                  const((1, D)), const(w_rkv.shape), const(w_k.shape), const(w_qvt.shape), const(w_gate.shape),
                  const((1, W3)), const(mu_wag.shape),
                  const(w1.shape), const(w2.shape), const((1, W)),
                  const(a1.shape), const(a2.shape), const((1, W)),
                  const(g1.shape), const(g2.shape)],
        out_specs=[rows(W3), rows(W), rows(W), rows(W), rows(S), feat_major, feat_major, rows(w_gate.shape[1])],
        out_shape=[jax.ShapeDtypeStruct((M, W3), F32),
                   jax.ShapeDtypeStruct((M, W), F32),
                   jax.ShapeDtypeStruct((M, W), F32),
                   jax.ShapeDtypeStruct((M, W), F32),
                   jax.ShapeDtypeStruct((M, S), BF16),
                   jax.ShapeDtypeStruct((M // SB_TILE, S, SB_TILE), BF16),
                   jax.ShapeDtypeStruct((M // SB_TILE, S, SB_TILE), BF16),
                   jax.ShapeDtypeStruct((M, w_gate.shape[1]), BF16)],
        scratch_shapes=[pltpu.VMEM((1, D), F32), pltpu.VMEM((1, W3), F32)],
        compiler_params=_cparams(("arbitrary",)),
        name="inproj",
    )(x2, mod3, norm_g.reshape(1, D), w_rkv, w_k, w_qvt, w_gate, mu_rkv.reshape(1, W3), mu_wag,
      w1, w2, w0.reshape(1, W), a1, a2, a0.reshape(1, W), g1, g2)


def _rwkv_kernel(r_ref, k_ref, v_ref, ld_ref, ai_ref, g_ref, kk_ref, ka_ref, rk_ref, lw_ref, lb_ref,
                 y_ref, state_ref, *, chunk):
    C = chunk
    tb, W = r_ref.shape
    n_slab = W // LANES
    n_chunks = tb // C
    S2 = 2 * C

    @pl.when(pl.program_id(1) == 0)
    def _():
        state_ref[...] = jnp.zeros_like(state_ref)

    lane = lax.broadcasted_iota(jnp.int32, (1, LANES), 1)
    head0 = lane < HEAD_DIM
    li = lax.broadcasted_iota(jnp.int32, (LANES, LANES), 0)
    lj = lax.broadcasted_iota(jnp.int32, (LANES, LANES), 1)
    seg_ones = ((li // HEAD_DIM) == (lj // HEAD_DIM)).astype(BF16)
    n_waves = 2 if n_chunks % 2 == 0 and n_chunks >= 8 else 1
    wave_chunks = n_chunks // n_waves
    tw = wave_chunks * C
    ti = lax.broadcasted_iota(jnp.int32, (tw, tw), 0)
    tj = lax.broadcasted_iota(jnp.int32, (tw, tw), 1)
    cum_ones = ((ti // C == tj // C) & (tj <= ti)).astype(BF16)
    si = lax.broadcasted_iota(jnp.int32, (S2, S2), 0)
    sj = lax.broadcasted_iota(jnp.int32, (S2, S2), 1)
    same = (si // C) == (sj // C)
    strict = same & (sj < si)
    incl = same & (sj <= si)
    eye = si == sj
    level_masks = []
    bsz = 1
    while bsz < C:
        level_masks.append(((si // (2 * bsz)) == (sj // (2 * bsz)))
                           & (((si // bsz) % 2) == 1) & (((sj // bsz) % 2) == 0))
        bsz *= 2

    def seg_sum(x):
        return _dg(x.astype(BF16), seg_ones)

    def stack(x):
        return jnp.concatenate([jnp.where(head0, x, 0.0), jnp.where(head0, 0.0, x)], axis=0)

    zeros = jnp.zeros((S2, LANES), BF16)
    rows = lambda c: slice(c * C, (c + 1) * C)
    cols = lambda p: slice(p * LANES, (p + 1) * LANES)

    def odd_rows(x, b):
        return jnp.concatenate([x[i * b:(i + 1) * b] for i in range(1, S2 // b, 2)], axis=0)

    def add_to_odd_rows(x, upd, b):
        return jnp.concatenate([x[i * b:(i + 1) * b] + upd[(i // 2) * b:(i // 2 + 1) * b] if i % 2 else
                                x[i * b:(i + 1) * b] for i in range(S2 // b)], axis=0)

    def prepare(p, w):
        tr = slice(w * tw, (w + 1) * tw)
        ld = ld_ref[tr, cols(p)]
        ai = ai_ref[tr, cols(p)]
        k = k_ref[tr, cols(p)]
        r = r_ref[tr, cols(p)]
        kk = k * kk_ref[:, cols(p)]
        kk = kk / jnp.maximum(jnp.sqrt(seg_sum(kk * kk)), 1e-12)
        cum = _mm(cum_ones, ld, nb=2)
        e_in = jnp.exp(cum)
        e_neg = jnp.exp(-cum)
        k2 = k * (1.0 + (ai - 1.0) * ka_ref[:, cols(p)])
        return dict(r=r, v=v_ref[tr, cols(p)], k2=k2, e_in=e_in, a_t=-kk * jnp.exp(cum - ld), r_t=r * e_in,
                    b_t=kk * ai * e_neg, k_t=k2 * e_neg)

    def chunk_products(tok, units, out):
        n = range(len(units))
        p_end = [tok[p]["e_in"][(c + 1) * C - 1:(c + 1) * C, :] for p, c in units]
        a_s = [stack(tok[p]["a_t"][rows(c)]).astype(BF16) for p, c in units]
        r_s = [stack(tok[p]["r_t"][rows(c)]) for p, c in units]
        v_s = [stack(tok[p]["v"][rows(c)]).astype(BF16) for p, c in units]
        bk = [jnp.concatenate([stack(tok[p]["b_t"][rows(c)]), stack(tok[p]["k_t"][rows(c)])], axis=0)
              for p, c in units]
        sc = [_dg(jnp.concatenate([a_s[u], r_s[u].astype(BF16)], axis=0), bk[u].astype(BF16), _NT) for u in n]
        yield
        a_ab = [jnp.where(strict, sc[u][:S2, :S2], 0.0) for u in n]
        a_ak = [jnp.where(strict, sc[u][:S2, S2:], 0.0).astype(BF16) for u in n]
        a_rb = [jnp.where(incl, sc[u][S2:, :S2], 0.0).astype(BF16) for u in n]
        a_rk = [jnp.where(incl, sc[u][S2:, S2:], 0.0).astype(BF16) for u in n]
        pinv = [jnp.where(eye, 1.0, jnp.where(level_masks[0], a_ab[u], 0.0)) for u in n]
        for lvl, mask in enumerate(level_masks[1:], start=1):
            b = 1 << lvl
            pb = [pinv[u].astype(BF16) for u in n]
            lower = [jnp.where(mask, a_ab[u], 0.0).astype(BF16) for u in n]
            if b % 16 == 0:
                xl = [_dg(odd_rows(pb[u], b), lower[u]) for u in n]
                yield
                pinv = [add_to_odd_rows(pinv[u], _dg(xl[u].astype(BF16), pb[u]), b) for u in n]
            else:
                xl = [_dg(pb[u], lower[u]) for u in n]
                yield
                pinv = [pinv[u] + _dg(xl[u].astype(BF16), pb[u]) for u in n]
            yield
        av = [_dg(a_ak[u], v_s[u]) for u in n]
        yield
        ta = [_dg(pinv[u].astype(BF16), jnp.concatenate([a_s[u], av[u].astype(BF16)], axis=1)).astype(BF16)
              for u in n]
        yield
        rhs = [jnp.concatenate([ta[u], jnp.concatenate([zeros, v_s[u]], axis=1)], axis=0) for u in n]
        out_side = [_dg(jnp.concatenate([a_rb[u], a_rk[u]], axis=1), rhs[u]) for u in n]
        state_side = [_dg((bk[u] * p_end[u]).astype(BF16), rhs[u], _TN) for u in n]
        for u, key in enumerate(units):
            out[key] = ((r_s[u] + out_side[u][:, :LANES]).astype(BF16),
                        out_side[u][:, LANES:],
                        (jnp.where(eye, p_end[u], 0.0) + state_side[u][:, :LANES]).astype(BF16),
                        state_side[u][:, LANES:])

    def finish(tok, w, prod):
        tr = slice(w * tw, (w + 1) * tw)
        state = {p: state_ref[p] for p in slabs}
        ys = {p: [] for p in slabs}
        for c in range(wave_chunks):
            for p in slabs:
                r_h, y_h, m_mat, g_mat = prod[(p, c)]
                sb = state[p].astype(BF16)
                y_st = _dg(r_h, sb) + y_h
                ys[p].append(y_st[:C] + y_st[C:])
                state[p] = _dg(m_mat, sb) + g_mat
            yield
        inv_n = 1.0 / HEAD_DIM
        for p in slabs:
            state_ref[p] = state[p]
            y = jnp.concatenate(ys[p], axis=0) if wave_chunks > 1 else ys[p][0]
            mean = seg_sum(y) * inv_n
            yc = y - mean
            var = seg_sum(yc * yc) * inv_n
            yn = yc * lax.rsqrt(var + LNX_EPS) * lw_ref[:, cols(p)] + lb_ref[:, cols(p)]
            bonus = seg_sum(tok[p]["r"] * tok[p]["k2"] * rk_ref[:, cols(p)]) * tok[p]["v"]
            y_ref[tr, cols(p)] = (yn + bonus) * g_ref[tr, cols(p)]
            yield

    def chain(gens):
        for g in gens:
            yield from g

    slabs = list(range(n_slab))
    tok = [dict() for _ in range(n_waves)]
    prod = [dict() for _ in range(n_waves)]

    def prepare_wave(w):
        for p in slabs:
            tok[w][p] = prepare(p, w)
            yield

    side = prepare_wave(0)
    for w in range(n_waves):
        for _ in side:
            pass
        side = chain(([finish(tok[w - 1], w - 1, prod[w - 1])] if w > 0 else [])
                     + ([prepare_wave(w + 1)] if w + 1 < n_waves else []))
        for _ in chunk_products(tok[w], [(p, c) for c in range(wave_chunks) for p in slabs], prod[w]):
            next(side, None)
    for _ in chain([side, finish(tok[n_waves - 1], n_waves - 1, prod[n_waves - 1])]):
        pass


def _rwkv_call(rkv, ld, ai, g, k_k, k_a, r_k, lnx_w, lnx_b, *, batch, seq, tb):
    M, W = ld.shape
    steps = seq // tb
    tok = lambda off: pl.BlockSpec((tb, W), lambda b, s: (b * steps + s, off))
    par = pl.BlockSpec((1, W), lambda b, s: (0, 0))
    return pl.pallas_call(
        functools.partial(_rwkv_kernel, chunk=RWKV_CHUNK),
        grid=(batch, steps),
        in_specs=[tok(0), tok(1), tok(2), tok(0), tok(0), tok(0), par, par, par, par, par],
        out_specs=tok(0),
        out_shape=jax.ShapeDtypeStruct((M, W), F32),
        scratch_shapes=[pltpu.VMEM((W // LANES, LANES, LANES), F32)],
        compiler_params=_cparams(("arbitrary", "arbitrary")),
        name="rwkv",
    )(rkv, rkv, rkv, ld, ai, g, k_k.reshape(1, W), k_a.reshape(1, W), r_k.reshape(1, W),
      lnx_w.reshape(1, W), lnx_b.reshape(1, W))


def _stick_kernel(qt_ref, k_ref, vt_ref, o_ref, acc_ref, run_ref, *, tile, group):
    first = pl.program_id(2) * group
    si = lax.broadcasted_iota(jnp.int32, (tile, 2 * tile), 0)
    ti = lax.broadcasted_iota(jnp.int32, (tile, 2 * tile), 1) % tile
    past = si < ti
    ui = lax.broadcasted_iota(jnp.int32, (tile, tile), 0)
    uj = lax.broadcasted_iota(jnp.int32, (tile, tile), 1)
    later_ones = (uj > ui).astype(BF16)
    feat0 = lax.broadcasted_iota(jnp.int32, (LANES, tile), 0) < HEAD_DIM
    zero = jnp.zeros((LANES, tile), BF16)
    q_pair = [jnp.concatenate([jnp.where(feat0, qt_ref[s], zero), jnp.where(feat0, zero, qt_ref[s])], axis=1)
              for s in range(group)]
    n = range(group)

    def visit(d, diagonal):
        kb, vt = [], []
        for s in n:
            jb = first + s - d
            blk = jnp.maximum(jb, 0)
            kb.append(k_ref[pl.ds(pl.multiple_of(blk * tile, tile), tile), :])
            vt.append(vt_ref[blk] if diagonal else jnp.where(jb >= 0, vt_ref[blk], zero))
        z = [_dg(kb[s], q_pair[s]) for s in n]
        ls = [_log_sigmoid(x) for x in z]
        lk = [ls[s] - z[s] for s in n]
        if diagonal:
            lk = [jnp.where(past, x, 0.0) for x in lk]
        later = [_dg(later_ones, lk[s].astype(BF16)) for s in n]
        worst = None
        for s in n:
            total = later[s][0:1, :] + lk[s][0:1, :]
            if diagonal:
                w = jnp.where(past, jnp.exp(ls[s] + later[s]), 0.0)
                run = total
            else:
                run = run_ref[s]
                w = jnp.exp(ls[s] + later[s] + run)
                run = run + total
            pv = _dg(vt[s], w.astype(BF16))
            pv = jnp.concatenate([pv[:HEAD_DIM, :tile], pv[HEAD_DIM:, tile:]], axis=0)
            acc_ref[s] = pv if diagonal else acc_ref[s] + pv
            run_ref[s] = run
            worst = run if worst is None else jnp.maximum(worst, run)
        return worst

    worst = visit(0, True)
    for d in range(1, SB_FIXED_VISITS):
        worst = visit(d, False)

    def cond(carry):
        d, worst = carry
        return (d < first + group) & (worst > -EXP_UNDERFLOW)

    def body(carry):
        d, _ = carry
        for j in range(SB_VISITS_PER_TEST):
            worst = visit(d + j, False)
        return d + SB_VISITS_PER_TEST, jnp.max(worst)

    lax.while_loop(cond, body, (jnp.int32(SB_FIXED_VISITS), jnp.max(worst)))
    for s in n:
        o_ref[s * tile:(s + 1) * tile, :] = acc_ref[s].T.astype(o_ref.dtype)


def _stick_call(ksb, qt, vt, *, batch, seq, group):
    M, W = ksb.shape
    n_slab = W // LANES
    tile = SB_TILE
    rows = tile * group
    steps = seq // rows
    blocks = seq // tile
    return pl.pallas_call(
        functools.partial(_stick_kernel, tile=tile, group=group),
        grid=(batch, n_slab, steps),
        in_specs=[pl.BlockSpec((group, LANES, tile), lambda b, p, i: (b * steps + i, p, 0)),
                  pl.BlockSpec((seq, LANES), lambda b, p, i: (b, p)),
                  pl.BlockSpec((blocks, LANES, tile), lambda b, p, i: (b, p, 0))],
        out_specs=pl.BlockSpec((rows, LANES), lambda b, p, i: (b * steps + i, p)),
        out_shape=jax.ShapeDtypeStruct((M, W), BF16),
        scratch_shapes=[pltpu.VMEM((group, LANES, tile), F32), pltpu.VMEM((group, 1, 2 * tile), F32)],
        compiler_params=_cparams(("arbitrary", "arbitrary", "arbitrary")),
        name="stick",
    )(qt, ksb, vt)


def _post_kernel(x_ref, ya_ref, yb_ref, gate_ref, mod_ref, wa_ref, wb_ref, wo_ref, g2_ref, gf_ref,
                 wg_ref, wu_ref, wd_ref, o_ref, *, final_norm):
    D = x_ref.shape[1]
    gt1, sh2, sc2, gt2 = (mod_ref[0, j:j + 1, :] for j in (2, 3, 4, 5))
    pa = _dg(ya_ref[...].astype(BF16), wa_ref[...])
    pb = _dg(yb_ref[...], wb_ref[...])
    merged = gate_ref[:, :D].astype(F32) * pa + gate_ref[:, D:].astype(F32) * pb
    x1 = x_ref[...] + gt1 * _dg(merged.astype(BF16), wo_ref[...])

    ms = jnp.mean(x1 * x1, axis=-1, keepdims=True)
    hb = ((x1 * lax.rsqrt(ms + NORM_EPS) * g2_ref[...]) * (1.0 + sc2) + sh2).astype(BF16)
    ug = _dg(hb, wg_ref[...])
    uu = _dg(hb, wu_ref[...])
    act = (ug * _sigmoid(ug) * uu).astype(BF16)
    x2 = x1 + gt2 * _dg(act, wd_ref[...])
    if final_norm:
        ms = jnp.mean(x2 * x2, axis=-1, keepdims=True)
        x2 = x2 * lax.rsqrt(ms + NORM_EPS) * gf_ref[...]
    o_ref[...] = x2


def _post_call(x2, ya, yb, gates, mod3, w_pa, w_pb, w_out, norm2_g, final_g, w_g, w_u, w_d,
               *, seq, tm, final_norm):
    M, D = x2.shape
    tiles_per_seq = seq // tm
    const = _const_spec
    rows = lambda n: pl.BlockSpec((tm, n), lambda i: (i, 0))
    return pl.pallas_call(
        functools.partial(_post_kernel, final_norm=final_norm),
        grid=(M // tm,),
        in_specs=[rows(D), rows(ya.shape[1]), rows(yb.shape[1]), rows(gates.shape[1]),
                  pl.BlockSpec((1, mod3.shape[1], D), lambda i: (i // tiles_per_seq, 0, 0)),
                  const(w_pa.shape), const(w_pb.shape), const(w_out.shape), const((1, D)), const((1, D)),
                  const(w_g.shape), const(w_u.shape), const(w_d.shape)],
        out_specs=rows(D),
        out_shape=jax.ShapeDtypeStruct((M, D), F32),
        compiler_params=_cparams(("arbitrary",)),
        name="post",
    )(x2, ya, yb, gates, mod3, w_pa, w_pb, w_out, norm2_g.reshape(1, D), final_g.reshape(1, D), w_g, w_u, w_d)


def _tiles(seq):
    pick = lambda want: max(t for t in (8, 16, 32, 64, 128, 256, 512, 1024) if t <= want and seq % t == 0)
    return dict(inproj=pick(512), rwkv=pick(512), stick=pick(SB_GROUP * SB_TILE), post=pick(512))


def kernel(x, c, w_ada, b_ada, norm1_g, w_in, mu_rkv, mu_wag, w0, w1, w2, a0, a1, a2, g1, g2, k_k, k_a, r_k,
           lnx_w, lnx_b, w_proj_a, w_proj_b, w_out, norm2_g, w_ffn_in, w_ffn_out, final_g):
    B, T, D = x.shape
    depth = w_ada.shape[0]
    W = w0.shape[1]
    S = w_proj_b.shape[1]
    F = w_ffn_out.shape[1]
    assert T % SB_TILE == 0 and W % LANES == 0 and S % LANES == 0
    tiles = _tiles(T)
    x2 = x.reshape(B * T, D)
    for l in range(depth):
        mod3 = _mod_call(c, w_ada[l], b_ada[l]).reshape(B, 6, D)
        wl = w_in[l].astype(BF16)
        w_q, w_k, w_v = (wl[:, 3 * W + j * S:3 * W + (j + 1) * S] for j in range(3))
        rkv, ld, ai, g, ksb, qt, vt, gates = _inproj_call(
            x2, mod3, norm1_g[l], wl[:, :3 * W], w_k, jnp.concatenate([w_q, w_v], axis=1).T,
            wl[:, 3 * W + 3 * S:],
            mu_rkv[l], mu_wag[l], w1[l].astype(BF16), w2[l].astype(BF16), w0[l],
            a1[l].astype(BF16), a2[l].astype(BF16), a0[l], g1[l].astype(BF16), g2[l].astype(BF16),
            seq=T, tm=tiles["inproj"])
        ya = _rwkv_call(rkv, ld, ai, g, k_k[l], k_a[l], r_k[l], lnx_w[l], lnx_b[l],
                        batch=B, seq=T, tb=tiles["rwkv"])
        yb = _stick_call(ksb, qt, vt, batch=B, seq=T, group=tiles["stick"] // SB_TILE)
        wf = w_ffn_in[l].astype(BF16)
        x2 = _post_call(x2, ya, yb, gates, mod3, w_proj_a[l].astype(BF16), w_proj_b[l].astype(BF16),
                        w_out[l].astype(BF16), norm2_g[l], final_g, wf[:, :F], wf[:, F:],
                        w_ffn_out[l].astype(BF16), seq=T, tm=tiles["post"], final_norm=(l == depth - 1))
    return x2.reshape(B, T, D)
```

```python
import functools

import jax
import jax.numpy as jnp
from jax import lax
from jax.experimental import pallas as pl
from jax.experimental.pallas import tpu as pltpu

F32 = jnp.float32
BF16 = jnp.bfloat16

HEAD_DIM = 64
LANES = 128
NORM_EPS = 1e-6
LNX_EPS = 64e-5
RWKV_CHUNK = 64
SB_TILE = 128
SB_Q_TILES = 2
SB_FIXED_VISITS = 4
SB_VISITS_PER_TEST = 2
EXP_UNDERFLOW = 104.0
VMEM_LIMIT = 56 * 1024 * 1024


def _cparams(sem):
    return pltpu.CompilerParams(dimension_semantics=sem, vmem_limit_bytes=VMEM_LIMIT)


def _const_spec(shape):
    return pl.BlockSpec(shape, lambda i: (0,) * len(shape), pipeline_mode=pl.Buffered(1))


_NN = (((1,), (0,)), ((), ()))
_NT = (((1,), (1,)), ((), ()))
_TN = (((0,), (0,)), ((), ()))


def _dg(a, b, dn=_NN):
    return lax.dot_general(a, b, dn, preferred_element_type=F32)


def _split(x, n):
    pieces, rem = [], x
    for i in range(n):
        p = rem.astype(BF16)
        pieces.append(p)
        if i + 1 < n:
            rem = rem - p.astype(F32)
    return pieces


def _mm(a, b, dn=_NN, na=1, nb=1):
    pa = _split(a, na) if a.dtype != BF16 else [a]
    pb = _split(b, nb) if b.dtype != BF16 else [b]
    order = max(len(pa), len(pb))
    acc = None
    for i, x in enumerate(pa):
        for j, y in enumerate(pb):
            if i + j < order:
                t = _dg(x, y, dn)
                acc = t if acc is None else acc + t
    return acc


def _sigmoid(x):
    return 1.0 / (1.0 + jnp.exp(-x))


def _log_sigmoid(x):
    return jnp.minimum(x, 0.0) - jnp.log(1.0 + jnp.exp(-jnp.abs(x)))


def _mod_kernel(c_ref, w_ref, b_ref, o_ref):
    c = c_ref[...]
    c_act = c * _sigmoid(c)
    o_ref[...] = _mm(c_act, w_ref[...], na=2, nb=2) + b_ref[...]


def _mod_call(c, w_ada, b_ada):
    B, D = c.shape
    n = w_ada.shape[1]
    return pl.pallas_call(
        _mod_kernel,
        grid=(n // D,),
        in_specs=[pl.BlockSpec((B, D), lambda j: (0, 0)),
                  pl.BlockSpec((D, D), lambda j: (0, j)),
                  pl.BlockSpec((1, D), lambda j: (0, j))],
        out_specs=pl.BlockSpec((B, D), lambda j: (0, j)),
        out_shape=jax.ShapeDtypeStruct((B, n), F32),
        compiler_params=_cparams(("arbitrary",)),
        name="mod",
    )(c, w_ada, b_ada.reshape(1, n))


def _shift_rows(cur, prev_row):
    rolled = pltpu.roll(cur, 1, 0)
    row = lax.broadcasted_iota(jnp.int32, cur.shape, 0)
    return jnp.where(row == 0, prev_row, rolled)


def _inproj_kernel(x_ref, mod_ref, g_ref, wrkv_ref, wk_ref, wqvt_ref, wgate_ref, murkv_ref, muwag_ref,
                   w1_ref, w2_ref, w0_ref, a1_ref, a2_ref, a0_ref, g1_ref, g2_ref,
                   rkv_ref, ld_ref, ai_ref, go_ref, ksb_ref, qt_ref, vt_ref, gate_ref, hlast_ref, plast_ref,
                   *, tiles_per_seq):
    tm = x_ref.shape[0]

    @pl.when((pl.program_id(0) % tiles_per_seq) == 0)
    def _():
        hlast_ref[...] = jnp.zeros_like(hlast_ref)
        plast_ref[...] = jnp.zeros_like(plast_ref)

    shift = mod_ref[0, 0:1, :]
    scale = mod_ref[0, 1:2, :]
    ms = jnp.mean(x_ref[...] * x_ref[...], axis=-1, keepdims=True)
    h = (x_ref[...] * lax.rsqrt(ms + NORM_EPS) * g_ref[...]) * (1.0 + scale) + shift
    hb = h.astype(BF16)

    gate_ref[...] = _sigmoid(_dg(hb, wgate_ref[...])).astype(BF16)

    xx = _shift_rows(h, hlast_ref[...]) - h
    hlast_ref[...] = h[tm - 1:tm, :]
    xw = (h + xx * muwag_ref[0:1, :]).astype(BF16)
    xa = (h + xx * muwag_ref[1:2, :]).astype(BF16)
    xg = (h + xx * muwag_ref[2:3, :]).astype(BF16)
    lw = _dg(xw, w1_ref[...])
    la = _dg(xa, a1_ref[...])
    lg = _dg(xg, g1_ref[...])

    p = _dg(hb, wrkv_ref[...])
    rkv_ref[...] = p + (_shift_rows(p, plast_ref[...]) - p) * murkv_ref[...]
    plast_ref[...] = p[tm - 1:tm, :]

    u = w0_ref[...] + _dg(jnp.tanh(lw).astype(BF16), w2_ref[...])
    w_log = _log_sigmoid(u) - 0.5
    ld_ref[...] = -jnp.exp(w_log)
    ai_ref[...] = _sigmoid(a0_ref[...] + _dg(la.astype(BF16), a2_ref[...]))
    go_ref[...] = _dg(_sigmoid(lg).astype(BF16), g2_ref[...])

    ksb_ref[...] = _dg(hb, wk_ref[...]).astype(BF16)
    qv = _dg(wqvt_ref[...], hb, _NT)
    n_feat = qt_ref.shape[1]
    for blk in range(x_ref.shape[0] // SB_TILE):
        cols = slice(blk * SB_TILE, (blk + 1) * SB_TILE)
        qt_ref[blk] = (qv[:n_feat, cols] * (HEAD_DIM ** -0.5)).astype(BF16)
        vt_ref[blk] = qv[n_feat:, cols].astype(BF16)


def _inproj_call(x2, mod3, norm_g, w_rkv, w_k, w_qvt, w_gate, mu_rkv, mu_wag, w1, w2, w0, a1, a2, a0, g1, g2,
                 *, seq, tm):
    M, D = x2.shape
    W3 = w_rkv.shape[1]
    W = W3 // 3
    S = w_k.shape[1]
    tiles_per_seq = seq // tm
    const = _const_spec
    rows = lambda n: pl.BlockSpec((tm, n), lambda i: (i, 0))
    feat_major = pl.BlockSpec((tm // SB_TILE, S, SB_TILE), lambda i: (i, 0, 0))
    return pl.pallas_call(
        functools.partial(_inproj_kernel, tiles_per_seq=tiles_per_seq),
        grid=(M // tm,),
        in_specs=[rows(D),
                  pl.BlockSpec((1, mod3.shape[1], D), lambda i: (i // tiles_per_seq, 0, 0)),
                  const((1, D)), const(w_rkv.shape), const(w_k.shape), const(w_qvt.shape), const(w_gate.shape),
                  const((1, W3)), const(mu_wag.shape),
                  const(w1.shape), const(w2.shape), const((1, W)),
                  const(a1.shape), const(a2.shape), const((1, W)),
                  const(g1.shape), const(g2.shape)],
        out_specs=[rows(W3), rows(W), rows(W), rows(W), rows(S), feat_major, feat_major, rows(w_gate.shape[1])],
        out_shape=[jax.ShapeDtypeStruct((M, W3), F32),
                   jax.ShapeDtypeStruct((M, W), F32),
                   jax.ShapeDtypeStruct((M, W), F32),
                   jax.ShapeDtypeStruct((M, W), F32),
                   jax.ShapeDtypeStruct((M, S), BF16),
                   jax.ShapeDtypeStruct((M // SB_TILE, S, SB_TILE), BF16),
                   jax.ShapeDtypeStruct((M // SB_TILE, S, SB_TILE), BF16),
                   jax.ShapeDtypeStruct((M, w_gate.shape[1]), BF16)],
        scratch_shapes=[pltpu.VMEM((1, D), F32), pltpu.VMEM((1, W3), F32)],
        compiler_params=_cparams(("arbitrary",)),
        name="inproj",
    )(x2, mod3, norm_g.reshape(1, D), w_rkv, w_k, w_qvt, w_gate, mu_rkv.reshape(1, W3), mu_wag,
      w1, w2, w0.reshape(1, W), a1, a2, a0.reshape(1, W), g1, g2)


def _rwkv_kernel(r_ref, k_ref, v_ref, ld_ref, ai_ref, g_ref, kk_ref, ka_ref, rk_ref, lw_ref, lb_ref,
                 y_ref, state_ref, *, chunk):
    C = chunk
    tb, W = r_ref.shape
    n_slab = W // LANES
    n_chunks = tb // C
    S2 = 2 * C

    @pl.when(pl.program_id(1) == 0)
    def _():
        state_ref[...] = jnp.zeros_like(state_ref)

    lane = lax.broadcasted_iota(jnp.int32, (1, LANES), 1)
    head0 = lane < HEAD_DIM
    li = lax.broadcasted_iota(jnp.int32, (LANES, LANES), 0)
    lj = lax.broadcasted_iota(jnp.int32, (LANES, LANES), 1)
    seg_ones = ((li // HEAD_DIM) == (lj // HEAD_DIM)).astype(BF16)
    n_waves = 2 if n_chunks % 2 == 0 and n_chunks >= 8 else 1
    wave_chunks = n_chunks // n_waves
    tw = wave_chunks * C
    ti = lax.broadcasted_iota(jnp.int32, (tw, tw), 0)
    tj = lax.broadcasted_iota(jnp.int32, (tw, tw), 1)
    cum_ones = ((ti // C == tj // C) & (tj <= ti)).astype(BF16)
    si = lax.broadcasted_iota(jnp.int32, (S2, S2), 0)
    sj = lax.broadcasted_iota(jnp.int32, (S2, S2), 1)
    same = (si // C) == (sj // C)
    strict = same & (sj < si)
    incl = same & (sj <= si)
    eye = si == sj
    level_masks = []
    bsz = 1
    while bsz < C:
        level_masks.append(((si // (2 * bsz)) == (sj // (2 * bsz)))
                           & (((si // bsz) % 2) == 1) & (((sj // bsz) % 2) == 0))
        bsz *= 2

    def seg_sum(x):
        return _dg(x.astype(BF16), seg_ones)

    def stack(x):
        return jnp.concatenate([jnp.where(head0, x, 0.0), jnp.where(head0, 0.0, x)], axis=0)

    zeros = jnp.zeros((S2, LANES), BF16)
    rows = lambda c: slice(c * C, (c + 1) * C)
    cols = lambda p: slice(p * LANES, (p + 1) * LANES)

    def odd_rows(x, b):
        return jnp.concatenate([x[i * b:(i + 1) * b] for i in range(1, S2 // b, 2)], axis=0)

    def add_to_odd_rows(x, upd, b):
        return jnp.concatenate([x[i * b:(i + 1) * b] + upd[(i // 2) * b:(i // 2 + 1) * b] if i % 2 else
                                x[i * b:(i + 1) * b] for i in range(S2 // b)], axis=0)

    def prepare(p, w):
        tr = slice(w * tw, (w + 1) * tw)
        ld = ld_ref[tr, cols(p)]
        ai = ai_ref[tr, cols(p)]
        k = k_ref[tr, cols(p)]
        r = r_ref[tr, cols(p)]
        kk = k * kk_ref[:, cols(p)]
        kk = kk / jnp.maximum(jnp.sqrt(seg_sum(kk * kk)), 1e-12)
        cum = _mm(cum_ones, ld, nb=2)
        e_in = jnp.exp(cum)
        e_neg = jnp.exp(-cum)
        k2 = k * (1.0 + (ai - 1.0) * ka_ref[:, cols(p)])
        return dict(r=r, v=v_ref[tr, cols(p)], k2=k2, e_in=e_in, a_t=-kk * jnp.exp(cum - ld), r_t=r * e_in,
                    b_t=kk * ai * e_neg, k_t=k2 * e_neg)

    def chunk_products(tok, units, out):
        n = range(len(units))
        p_end = [tok[p]["e_in"][(c + 1) * C - 1:(c + 1) * C, :] for p, c in units]
        a_s = [stack(tok[p]["a_t"][rows(c)]).astype(BF16) for p, c in units]
        r_s = [stack(tok[p]["r_t"][rows(c)]) for p, c in units]
        v_s = [stack(tok[p]["v"][rows(c)]).astype(BF16) for p, c in units]
        bk = [jnp.concatenate([stack(tok[p]["b_t"][rows(c)]), stack(tok[p]["k_t"][rows(c)])], axis=0)
              for p, c in units]
        sc = [_dg(jnp.concatenate([a_s[u], r_s[u].astype(BF16)], axis=0), bk[u].astype(BF16), _NT) for u in n]
        yield
        a_ab = [jnp.where(strict, sc[u][:S2, :S2], 0.0) for u in n]
        a_ak = [jnp.where(strict, sc[u][:S2, S2:], 0.0).astype(BF16) for u in n]
        a_rb = [jnp.where(incl, sc[u][S2:, :S2], 0.0).astype(BF16) for u in n]
        a_rk = [jnp.where(incl, sc[u][S2:, S2:], 0.0).astype(BF16) for u in n]
        pinv = [jnp.where(eye, 1.0, jnp.where(level_masks[0], a_ab[u], 0.0)) for u in n]
        for lvl, mask in enumerate(level_masks[1:], start=1):
            b = 1 << lvl
            pb = [pinv[u].astype(BF16) for u in n]
            lower = [jnp.where(mask, a_ab[u], 0.0).astype(BF16) for u in n]
            if b % 16 == 0:
                xl = [_dg(odd_rows(pb[u], b), lower[u]) for u in n]
                yield
                pinv = [add_to_odd_rows(pinv[u], _dg(xl[u].astype(BF16), pb[u]), b) for u in n]
            else:
                xl = [_dg(pb[u], lower[u]) for u in n]
                yield
                pinv = [pinv[u] + _dg(xl[u].astype(BF16), pb[u]) for u in n]
            yield
        av = [_dg(a_ak[u], v_s[u]) for u in n]
        yield
        ta = [_dg(pinv[u].astype(BF16), jnp.concatenate([a_s[u], av[u].astype(BF16)], axis=1)).astype(BF16)
              for u in n]
        yield
        rhs = [jnp.concatenate([ta[u], jnp.concatenate([zeros, v_s[u]], axis=1)], axis=0) for u in n]
        out_side = [_dg(jnp.concatenate([a_rb[u], a_rk[u]], axis=1), rhs[u]) for u in n]
        state_side = [_dg((bk[u] * p_end[u]).astype(BF16), rhs[u], _TN) for u in n]
        for u, key in enumerate(units):
            out[key] = ((r_s[u] + out_side[u][:, :LANES]).astype(BF16),
                        out_side[u][:, LANES:],
                        (jnp.where(eye, p_end[u], 0.0) + state_side[u][:, :LANES]).astype(BF16),
                        state_side[u][:, LANES:])

    def finish(tok, w, prod):
        tr = slice(w * tw, (w + 1) * tw)
        state = {p: state_ref[p] for p in slabs}
        ys = {p: [] for p in slabs}
        for c in range(wave_chunks):
            for p in slabs:
                r_h, y_h, m_mat, g_mat = prod[(p, c)]
                sb = state[p].astype(BF16)
                y_st = _dg(r_h, sb) + y_h
                ys[p].append(y_st[:C] + y_st[C:])
                state[p] = _dg(m_mat, sb) + g_mat
            yield
        inv_n = 1.0 / HEAD_DIM
        for p in slabs:
            state_ref[p] = state[p]
            y = jnp.concatenate(ys[p], axis=0) if wave_chunks > 1 else ys[p][0]
            mean = seg_sum(y) * inv_n
            yc = y - mean
            var = seg_sum(yc * yc) * inv_n
            yn = yc * lax.rsqrt(var + LNX_EPS) * lw_ref[:, cols(p)] + lb_ref[:, cols(p)]
            bonus = seg_sum(tok[p]["r"] * tok[p]["k2"] * rk_ref[:, cols(p)]) * tok[p]["v"]
            y_ref[tr, cols(p)] = (yn + bonus) * g_ref[tr, cols(p)]
            yield

    def chain(gens):
        for g in gens:
            yield from g

    slabs = list(range(n_slab))
    tok = [dict() for _ in range(n_waves)]
    prod = [dict() for _ in range(n_waves)]

    def prepare_wave(w):
        for p in slabs:
            tok[w][p] = prepare(p, w)
            yield

    side = prepare_wave(0)
    for w in range(n_waves):
        for _ in side:
            pass
        side = chain(([finish(tok[w - 1], w - 1, prod[w - 1])] if w > 0 else [])
                     + ([prepare_wave(w + 1)] if w + 1 < n_waves else []))
        for _ in chunk_products(tok[w], [(p, c) for c in range(wave_chunks) for p in slabs], prod[w]):
            next(side, None)
    for _ in chain([side, finish(tok[n_waves - 1], n_waves - 1, prod[n_waves - 1])]):
        pass


def _rwkv_call(rkv, ld, ai, g, k_k, k_a, r_k, lnx_w, lnx_b, *, batch, seq, tb):
    M, W = ld.shape
    steps = seq // tb
    tok = lambda off: pl.BlockSpec((tb, W), lambda b, s: (b * steps + s, off))
    par = pl.BlockSpec((1, W), lambda b, s: (0, 0))
    return pl.pallas_call(
        functools.partial(_rwkv_kernel, chunk=RWKV_CHUNK),
        grid=(batch, steps),
        in_specs=[tok(0), tok(1), tok(2), tok(0), tok(0), tok(0), par, par, par, par, par],
        out_specs=tok(0),
        out_shape=jax.ShapeDtypeStruct((M, W), F32),
        scratch_shapes=[pltpu.VMEM((W // LANES, LANES, LANES), F32)],
        compiler_params=_cparams(("arbitrary", "arbitrary")),
        name="rwkv",
    )(rkv, rkv, rkv, ld, ai, g, k_k.reshape(1, W), k_a.reshape(1, W), r_k.reshape(1, W),
      lnx_w.reshape(1, W), lnx_b.reshape(1, W))


def _stick_kernel(qt_ref, k_ref, vt_ref, o_ref, acc_ref, run_ref, *, tile, q_tiles):
    first = pl.program_id(1) * q_tiles
    n_slab = k_ref.shape[1] // LANES
    chains = [(t, p) for t in range(q_tiles) for p in range(n_slab)]
    slab = lambda p: slice(p * LANES, (p + 1) * LANES)
    si = lax.broadcasted_iota(jnp.int32, (tile, 2 * tile), 0)
    ti = lax.broadcasted_iota(jnp.int32, (tile, 2 * tile), 1) % tile
    past = si < ti
    ui = lax.broadcasted_iota(jnp.int32, (tile, tile), 0)
    uj = lax.broadcasted_iota(jnp.int32, (tile, tile), 1)
    later_ones = (uj > ui).astype(BF16)
    feat0 = lax.broadcasted_iota(jnp.int32, (LANES, tile), 0) < HEAD_DIM
    zero = jnp.zeros((LANES, tile), BF16)
    q_pair = [jnp.concatenate([jnp.where(feat0, qt_ref[t, slab(p), :], zero),
                               jnp.where(feat0, zero, qt_ref[t, slab(p), :])], axis=1) for t, p in chains]
    n = range(len(chains))

    def visit(d, diagonal):
        kb, vt = [], []
        for t, p in chains:
            jb = first + t - d
            blk = jnp.maximum(jb, 0)
            kb.append(k_ref[pl.ds(pl.multiple_of(blk * tile, tile), tile), slab(p)])
            vt.append(vt_ref[blk, slab(p), :] if diagonal else jnp.where(jb >= 0, vt_ref[blk, slab(p), :], zero))
        z = [_dg(kb[s], q_pair[s]) for s in n]
        ls = [_log_sigmoid(x) for x in z]
        lk = [ls[s] - z[s] for s in n]
        if diagonal:
            lk = [jnp.where(past, x, 0.0) for x in lk]
        later = [_dg(later_ones, lk[s].astype(BF16)) for s in n]
        worst = None
        for s in n:
            total = later[s][0:1, :] + lk[s][0:1, :]
            if diagonal:
                w = jnp.where(past, jnp.exp(ls[s] + later[s]), 0.0)
                run = total
            else:
                run = run_ref[s]
                w = jnp.exp(ls[s] + later[s] + run)
                run = run + total
            pv = _dg(vt[s], w.astype(BF16))
            pv = jnp.concatenate([pv[:HEAD_DIM, :tile], pv[HEAD_DIM:, tile:]], axis=0)
            acc_ref[s] = pv if diagonal else acc_ref[s] + pv
            run_ref[s] = run
            worst = run if worst is None else jnp.maximum(worst, run)
        return worst

    worst = visit(0, True)
    for d in range(1, SB_FIXED_VISITS):
        worst = visit(d, False)

    def cond(carry):
        d, worst = carry
        return (d < first + q_tiles) & (worst > -EXP_UNDERFLOW)

    def body(carry):
        d, _ = carry
        for j in range(SB_VISITS_PER_TEST):
            worst = visit(d + j, False)
        return d + SB_VISITS_PER_TEST, jnp.max(worst)

    lax.while_loop(cond, body, (jnp.int32(SB_FIXED_VISITS), jnp.max(worst)))
    for c, (t, p) in enumerate(chains):
        o_ref[t * tile:(t + 1) * tile, slab(p)] = acc_ref[c].T.astype(o_ref.dtype)


def _stick_call(ksb, qt, vt, *, batch, seq, q_tiles):
    M, W = ksb.shape
    tile = SB_TILE
    rows = tile * q_tiles
    steps = seq // rows
    blocks = seq // tile
    chains = q_tiles * (W // LANES)
    return pl.pallas_call(
        functools.partial(_stick_kernel, tile=tile, q_tiles=q_tiles),
        grid=(batch, steps),
        in_specs=[pl.BlockSpec((q_tiles, W, tile), lambda b, i: (b * steps + i, 0, 0)),
                  pl.BlockSpec((seq, W), lambda b, i: (b, 0)),
                  pl.BlockSpec((blocks, W, tile), lambda b, i: (b, 0, 0))],
        out_specs=pl.BlockSpec((rows, W), lambda b, i: (b * steps + i, 0)),
        out_shape=jax.ShapeDtypeStruct((M, W), BF16),
        scratch_shapes=[pltpu.VMEM((chains, LANES, tile), F32), pltpu.VMEM((chains, 1, 2 * tile), F32)],
        compiler_params=_cparams(("arbitrary", "arbitrary")),
        name="stick",
    )(qt, ksb, vt)


def _post_kernel(x_ref, ya_ref, yb_ref, gate_ref, mod_ref, wa_ref, wb_ref, wo_ref, g2_ref, gf_ref,
                 wg_ref, wu_ref, wd_ref, o_ref, *, final_norm):
    D = x_ref.shape[1]
    gt1, sh2, sc2, gt2 = (mod_ref[0, j:j + 1, :] for j in (2, 3, 4, 5))
    pa = _dg(ya_ref[...].astype(BF16), wa_ref[...])
    pb = _dg(yb_ref[...], wb_ref[...])
    merged = gate_ref[:, :D].astype(F32) * pa + gate_ref[:, D:].astype(F32) * pb
    x1 = x_ref[...] + gt1 * _dg(merged.astype(BF16), wo_ref[...])

    ms = jnp.mean(x1 * x1, axis=-1, keepdims=True)
    hb = ((x1 * lax.rsqrt(ms + NORM_EPS) * g2_ref[...]) * (1.0 + sc2) + sh2).astype(BF16)
    ug = _dg(hb, wg_ref[...])
    uu = _dg(hb, wu_ref[...])
    act = (ug * _sigmoid(ug) * uu).astype(BF16)
    x2 = x1 + gt2 * _dg(act, wd_ref[...])
    if final_norm:
        ms = jnp.mean(x2 * x2, axis=-1, keepdims=True)
        x2 = x2 * lax.rsqrt(ms + NORM_EPS) * gf_ref[...]
    o_ref[...] = x2


def _post_call(x2, ya, yb, gates, mod3, w_pa, w_pb, w_out, norm2_g, final_g, w_g, w_u, w_d,
               *, seq, tm, final_norm):
    M, D = x2.shape
    tiles_per_seq = seq // tm
    const = _const_spec
    rows = lambda n: pl.BlockSpec((tm, n), lambda i: (i, 0))
    return pl.pallas_call(
        functools.partial(_post_kernel, final_norm=final_norm),
        grid=(M // tm,),
        in_specs=[rows(D), rows(ya.shape[1]), rows(yb.shape[1]), rows(gates.shape[1]),
                  pl.BlockSpec((1, mod3.shape[1], D), lambda i: (i // tiles_per_seq, 0, 0)),
                  const(w_pa.shape), const(w_pb.shape), const(w_out.shape), const((1, D)), const((1, D)),
                  const(w_g.shape), const(w_u.shape), const(w_d.shape)],
        out_specs=rows(D),
        out_shape=jax.ShapeDtypeStruct((M, D), F32),
        compiler_params=_cparams(("arbitrary",)),
        name="post",
    )(x2, ya, yb, gates, mod3, w_pa, w_pb, w_out, norm2_g.reshape(1, D), final_g.reshape(1, D), w_g, w_u, w_d)


def _tiles(seq):
    pick = lambda want: max(t for t in (8, 16, 32, 64, 128, 256, 512, 1024) if t <= want and seq % t == 0)
    return dict(inproj=pick(512), rwkv=pick(512), stick=pick(SB_Q_TILES * SB_TILE), post=pick(512))


def kernel(x, c, w_ada, b_ada, norm1_g, w_in, mu_rkv, mu_wag, w0, w1, w2, a0, a1, a2, g1, g2, k_k, k_a, r_k,
           lnx_w, lnx_b, w_proj_a, w_proj_b, w_out, norm2_g, w_ffn_in, w_ffn_out, final_g):
    B, T, D = x.shape
    depth = w_ada.shape[0]
    W = w0.shape[1]
    S = w_proj_b.shape[1]
    F = w_ffn_out.shape[1]
    assert T % SB_TILE == 0 and W % LANES == 0 and S % LANES == 0
    tiles = _tiles(T)
    x2 = x.reshape(B * T, D)
    for l in range(depth):
        mod3 = _mod_call(c, w_ada[l], b_ada[l]).reshape(B, 6, D)
        wl = w_in[l].astype(BF16)
        w_q, w_k, w_v = (wl[:, 3 * W + j * S:3 * W + (j + 1) * S] for j in range(3))
        rkv, ld, ai, g, ksb, qt, vt, gates = _inproj_call(
            x2, mod3, norm1_g[l], wl[:, :3 * W], w_k, jnp.concatenate([w_q, w_v], axis=1).T,
            wl[:, 3 * W + 3 * S:],
            mu_rkv[l], mu_wag[l], w1[l].astype(BF16), w2[l].astype(BF16), w0[l],
            a1[l].astype(BF16), a2[l].astype(BF16), a0[l], g1[l].astype(BF16), g2[l].astype(BF16),
            seq=T, tm=tiles["inproj"])
        ya = _rwkv_call(rkv, ld, ai, g, k_k[l], k_a[l], r_k[l], lnx_w[l], lnx_b[l],
                        batch=B, seq=T, tb=tiles["rwkv"])
        yb = _stick_call(ksb, qt, vt, batch=B, seq=T, q_tiles=tiles["stick"] // SB_TILE)
        wf = w_ffn_in[l].astype(BF16)
        x2 = _post_call(x2, ya, yb, gates, mod3, w_proj_a[l].astype(BF16), w_proj_b[l].astype(BF16),
                        w_out[l].astype(BF16), norm2_g[l], final_g, wf[:, :F], wf[:, F:],
                        w_ffn_out[l].astype(BF16), seq=T, tm=tiles["post"], final_norm=(l == depth - 1))
    return x2.reshape(B, T, D)
```

```python
import functools

import jax
import jax.numpy as jnp
from jax import lax
from jax.experimental import pallas as pl
from jax.experimental.pallas import tpu as pltpu

F32 = jnp.float32
BF16 = jnp.bfloat16

HEAD_DIM = 64
LANES = 128
NORM_EPS = 1e-6
LNX_EPS = 64e-5
RWKV_CHUNK = 64
RWKV_WAVE_UNITS = 16
SB_TILE = 128
SB_GROUP = 8
SB_FIXED_VISITS = 4
SB_VISITS_PER_TEST = 2
EXP_UNDERFLOW = 104.0
VMEM_LIMIT = 56 * 1024 * 1024


def _cparams(sem):
    return pltpu.CompilerParams(dimension_semantics=sem, vmem_limit_bytes=VMEM_LIMIT)


def _const_spec(shape):
    return pl.BlockSpec(shape, lambda i: (0,) * len(shape), pipeline_mode=pl.Buffered(1))


_NN = (((1,), (0,)), ((), ()))
_NT = (((1,), (1,)), ((), ()))
_TN = (((0,), (0,)), ((), ()))


def _dg(a, b, dn=_NN):
    return lax.dot_general(a, b, dn, preferred_element_type=F32)


def _split(x, n):
    pieces, rem = [], x
    for i in range(n):
        p = rem.astype(BF16)
        pieces.append(p)
        if i + 1 < n:
            rem = rem - p.astype(F32)
    return pieces


def _mm(a, b, dn=_NN, na=1, nb=1):
    pa = _split(a, na) if a.dtype != BF16 else [a]
    pb = _split(b, nb) if b.dtype != BF16 else [b]
    order = max(len(pa), len(pb))
    acc = None
    for i, x in enumerate(pa):
        for j, y in enumerate(pb):
            if i + j < order:
                t = _dg(x, y, dn)
                acc = t if acc is None else acc + t
    return acc


def _sigmoid(x):
    return 1.0 / (1.0 + jnp.exp(-x))


def _log_sigmoid(x):
    return jnp.minimum(x, 0.0) - jnp.log(1.0 + jnp.exp(-jnp.abs(x)))


def _mod_kernel(c_ref, w_ref, b_ref, o_ref):
    c = c_ref[...]
    c_act = c * _sigmoid(c)
    o_ref[...] = _mm(c_act, w_ref[...], na=2, nb=2) + b_ref[...]


def _mod_call(c, w_ada, b_ada):
    B, D = c.shape
    n = w_ada.shape[1]
    return pl.pallas_call(
        _mod_kernel,
        grid=(n // D,),
        in_specs=[pl.BlockSpec((B, D), lambda j: (0, 0)),
                  pl.BlockSpec((D, D), lambda j: (0, j)),
                  pl.BlockSpec((1, D), lambda j: (0, j))],
        out_specs=pl.BlockSpec((B, D), lambda j: (0, j)),
        out_shape=jax.ShapeDtypeStruct((B, n), F32),
        compiler_params=_cparams(("arbitrary",)),
        name="mod",
    )(c, w_ada, b_ada.reshape(1, n))


def _shift_rows(cur, prev_row):
    rolled = pltpu.roll(cur, 1, 0)
    row = lax.broadcasted_iota(jnp.int32, cur.shape, 0)
    return jnp.where(row == 0, prev_row, rolled)


def _inproj_kernel(x_ref, mod_ref, g_ref, wrkv_ref, wk_ref, wqvt_ref, wgate_ref, murkv_ref, muwag_ref,
                   w1_ref, w2_ref, w0_ref, a1_ref, a2_ref, a0_ref, g1_ref, g2_ref,
                   rkv_ref, ld_ref, ai_ref, go_ref, ksb_ref, qt_ref, vt_ref, gate_ref, hlast_ref, plast_ref,
                   *, tiles_per_seq):
    tm = x_ref.shape[0]

    @pl.when((pl.program_id(0) % tiles_per_seq) == 0)
    def _():
        hlast_ref[...] = jnp.zeros_like(hlast_ref)
        plast_ref[...] = jnp.zeros_like(plast_ref)

    shift = mod_ref[0, 0:1, :]
    scale = mod_ref[0, 1:2, :]
    ms = jnp.mean(x_ref[...] * x_ref[...], axis=-1, keepdims=True)
    h = (x_ref[...] * lax.rsqrt(ms + NORM_EPS) * g_ref[...]) * (1.0 + scale) + shift
    hb = h.astype(BF16)

    gate_ref[...] = _sigmoid(_dg(hb, wgate_ref[...])).astype(BF16)

    xx = _shift_rows(h, hlast_ref[...]) - h
    hlast_ref[...] = h[tm - 1:tm, :]
    xw = (h + xx * muwag_ref[0:1, :]).astype(BF16)
    xa = (h + xx * muwag_ref[1:2, :]).astype(BF16)
    xg = (h + xx * muwag_ref[2:3, :]).astype(BF16)
    lw = _dg(xw, w1_ref[...])
    la = _dg(xa, a1_ref[...])
    lg = _dg(xg, g1_ref[...])

    p = _dg(hb, wrkv_ref[...])
    rkv_ref[...] = p + (_shift_rows(p, plast_ref[...]) - p) * murkv_ref[...]
    plast_ref[...] = p[tm - 1:tm, :]

    u = w0_ref[...] + _dg(jnp.tanh(lw).astype(BF16), w2_ref[...])
    w_log = _log_sigmoid(u) - 0.5
    ld_ref[...] = -jnp.exp(w_log)
    ai_ref[...] = _sigmoid(a0_ref[...] + _dg(la.astype(BF16), a2_ref[...]))
    go_ref[...] = _dg(_sigmoid(lg).astype(BF16), g2_ref[...])

    ksb_ref[...] = _dg(hb, wk_ref[...]).astype(BF16)
    qv = _dg(wqvt_ref[...], hb, _NT)
    n_feat = qt_ref.shape[1]
    for blk in range(x_ref.shape[0] // SB_TILE):
        cols = slice(blk * SB_TILE, (blk + 1) * SB_TILE)
        qt_ref[blk] = (qv[:n_feat, cols] * (HEAD_DIM ** -0.5)).astype(BF16)
        vt_ref[blk] = qv[n_feat:, cols].astype(BF16)


def _inproj_call(x2, mod3, norm_g, w_rkv, w_k, w_qvt, w_gate, mu_rkv, mu_wag, w1, w2, w0, a1, a2, a0, g1, g2,
                 *, seq, tm):
    M, D = x2.shape
    W3 = w_rkv.shape[1]
    W = W3 // 3
    S = w_k.shape[1]
    tiles_per_seq = seq // tm
    const = _const_spec
    rows = lambda n: pl.BlockSpec((tm, n), lambda i: (i, 0))
    feat_major = pl.BlockSpec((tm // SB_TILE, S, SB_TILE), lambda i: (i, 0, 0))
    return pl.pallas_call(
        functools.partial(_inproj_kernel, tiles_per_seq=tiles_per_seq),
        grid=(M // tm,),
        in_specs=[rows(D),
                  pl.BlockSpec((1, mod3.shape[1], D), lambda i: (i // tiles_per_seq, 0, 0)),
                  const((1, D)), const(w_rkv.shape), const(w_k.shape), const(w_qvt.shape), const(w_gate.shape),
                  const((1, W3)), const(mu_wag.shape),
                  const(w1.shape), const(w2.shape), const((1, W)),
                  const(a1.shape), const(a2.shape), const((1, W)),
                  const(g1.shape), const(g2.shape)],
        out_specs=[rows(W3), rows(W), rows(W), rows(W), rows(S), feat_major, feat_major, rows(w_gate.shape[1])],
        out_shape=[jax.ShapeDtypeStruct((M, W3), F32),
                   jax.ShapeDtypeStruct((M, W), F32),
                   jax.ShapeDtypeStruct((M, W), F32),
                   jax.ShapeDtypeStruct((M, W), F32),
                   jax.ShapeDtypeStruct((M, S), BF16),
                   jax.ShapeDtypeStruct((M // SB_TILE, S, SB_TILE), BF16),
                   jax.ShapeDtypeStruct((M // SB_TILE, S, SB_TILE), BF16),
                   jax.ShapeDtypeStruct((M, w_gate.shape[1]), BF16)],
        scratch_shapes=[pltpu.VMEM((1, D), F32), pltpu.VMEM((1, W3), F32)],
        compiler_params=_cparams(("arbitrary",)),
        name="inproj",
    )(x2, mod3, norm_g.reshape(1, D), w_rkv, w_k, w_qvt, w_gate, mu_rkv.reshape(1, W3), mu_wag,
      w1, w2, w0.reshape(1, W), a1, a2, a0.reshape(1, W), g1, g2)


def _rwkv_kernel(r_ref, k_ref, v_ref, ld_ref, ai_ref, g_ref, kk_ref, ka_ref, rk_ref, lw_ref, lb_ref,
                 y_ref, state_ref, *, chunk):
    C = chunk
    tb, W = r_ref.shape
    n_slab = W // LANES
    n_chunks = tb // C
    S2 = 2 * C

    @pl.when(pl.program_id(1) == 0)
    def _():
        state_ref[...] = jnp.zeros_like(state_ref)

    lane = lax.broadcasted_iota(jnp.int32, (1, LANES), 1)
    head0 = lane < HEAD_DIM
    li = lax.broadcasted_iota(jnp.int32, (LANES, LANES), 0)
    lj = lax.broadcasted_iota(jnp.int32, (LANES, LANES), 1)
    seg_ones = ((li // HEAD_DIM) == (lj // HEAD_DIM)).astype(BF16)
    li2 = lax.broadcasted_iota(jnp.int32, (2 * LANES, 2 * LANES), 0)
    lj2 = lax.broadcasted_iota(jnp.int32, (2 * LANES, 2 * LANES), 1)
    seg_ones2 = ((li2 // HEAD_DIM) == (lj2 // HEAD_DIM)).astype(BF16)
    n_waves = 2 if n_chunks % 2 == 0 and (n_chunks // 2) * n_slab >= RWKV_WAVE_UNITS else 1
    wave_chunks = n_chunks // n_waves
    tw = wave_chunks * C
    ti = lax.broadcasted_iota(jnp.int32, (tw, tw), 0)
    tj = lax.broadcasted_iota(jnp.int32, (tw, tw), 1)
    cum_ones = ((ti // C == tj // C) & (tj <= ti)).astype(BF16)
    si = lax.broadcasted_iota(jnp.int32, (S2, S2), 0)
    sj = lax.broadcasted_iota(jnp.int32, (S2, S2), 1)
    same = (si // C) == (sj // C)
    strict = same & (sj < si)
    incl = same & (sj <= si)
    eye = si == sj
    level_masks = []
    bsz = 1
    while bsz < C:
        level_masks.append(((si // (2 * bsz)) == (sj // (2 * bsz)))
                           & (((si // bsz) % 2) == 1) & (((sj // bsz) % 2) == 0))
        bsz *= 2

    def seg_sum(x):
        return _dg(x.astype(BF16), seg_ones)

    def stack(x):
        return jnp.concatenate([jnp.where(head0, x, 0.0), jnp.where(head0, 0.0, x)], axis=0)

    zeros = jnp.zeros((S2, LANES), BF16)
    rows = lambda c: slice(c * C, (c + 1) * C)
    cols = lambda p: slice(p * LANES, (p + 1) * LANES)

    def odd_rows(x, b):
        return jnp.concatenate([x[i * b:(i + 1) * b] for i in range(1, S2 // b, 2)], axis=0)

    def add_to_odd_rows(x, upd, b):
        return jnp.concatenate([x[i * b:(i + 1) * b] + upd[(i // 2) * b:(i // 2 + 1) * b] if i % 2 else
                                x[i * b:(i + 1) * b] for i in range(S2 // b)], axis=0)

    def prepare(p, w):
        tr = slice(w * tw, (w + 1) * tw)
        ld = ld_ref[tr, cols(p)]
        ai = ai_ref[tr, cols(p)]
        k = k_ref[tr, cols(p)]
        r = r_ref[tr, cols(p)]
        kk = k * kk_ref[:, cols(p)]
        k2 = k * (1.0 + (ai - 1.0) * ka_ref[:, cols(p)])
        sums = _dg(jnp.concatenate([kk * kk, r * k2 * rk_ref[:, cols(p)]], axis=1).astype(BF16), seg_ones2)
        kk = kk / jnp.maximum(jnp.sqrt(sums[:, :LANES]), 1e-12)
        cum = _mm(cum_ones, ld, nb=2)
        e_in = jnp.exp(cum)
        e_neg = jnp.exp(-cum)
        return dict(v=v_ref[tr, cols(p)], bonus=sums[:, LANES:], e_in=e_in, a_t=-kk * jnp.exp(cum - ld),
                    r_t=r * e_in, b_t=kk * ai * e_neg, k_t=k2 * e_neg)

    def chunk_products(tok, units, out):
        n = range(len(units))
        p_end = [tok[p]["e_in"][(c + 1) * C - 1:(c + 1) * C, :] for p, c in units]
        a_s = [stack(tok[p]["a_t"][rows(c)]).astype(BF16) for p, c in units]
        r_s = [stack(tok[p]["r_t"][rows(c)]) for p, c in units]
        v_s = [stack(tok[p]["v"][rows(c)]).astype(BF16) for p, c in units]
        bk = [jnp.concatenate([stack(tok[p]["b_t"][rows(c)]), stack(tok[p]["k_t"][rows(c)])], axis=0)
              for p, c in units]
        sc = [_dg(jnp.concatenate([a_s[u], r_s[u].astype(BF16)], axis=0), bk[u].astype(BF16), _NT) for u in n]
        yield
        a_ab = [jnp.where(strict, sc[u][:S2, :S2], 0.0) for u in n]
        a_ak = [jnp.where(strict, sc[u][:S2, S2:], 0.0).astype(BF16) for u in n]
        a_rb = [jnp.where(incl, sc[u][S2:, :S2], 0.0).astype(BF16) for u in n]
        a_rk = [jnp.where(incl, sc[u][S2:, S2:], 0.0).astype(BF16) for u in n]
        pinv = [jnp.where(eye, 1.0, jnp.where(level_masks[0], a_ab[u], 0.0)) for u in n]
        for lvl, mask in enumerate(level_masks[1:], start=1):
            b = 1 << lvl
            pb = [pinv[u].astype(BF16) for u in n]
            lower = [jnp.where(mask, a_ab[u], 0.0).astype(BF16) for u in n]
            if b % 16 == 0:
                xl = [_dg(odd_rows(pb[u], b), lower[u]) for u in n]
                yield
                pinv = [add_to_odd_rows(pinv[u], _dg(xl[u].astype(BF16), pb[u]), b) for u in n]
            else:
                xl = [_dg(pb[u], lower[u]) for u in n]
                yield
                pinv = [pinv[u] + _dg(xl[u].astype(BF16), pb[u]) for u in n]
            yield
        av = [_dg(a_ak[u], v_s[u]) for u in n]
        yield
        ta = [_dg(pinv[u].astype(BF16), jnp.concatenate([a_s[u], av[u].astype(BF16)], axis=1)).astype(BF16)
              for u in n]
        yield
        rhs = [jnp.concatenate([ta[u], jnp.concatenate([zeros, v_s[u]], axis=1)], axis=0) for u in n]
        out_side = [_dg(jnp.concatenate([a_rb[u], a_rk[u]], axis=1), rhs[u]) for u in n]
        state_side = [_dg((bk[u] * p_end[u]).astype(BF16), rhs[u], _TN) for u in n]
        for u, key in enumerate(units):
            out[key] = ((r_s[u] + out_side[u][:, :LANES]).astype(BF16),
                        out_side[u][:, LANES:],
                        (jnp.where(eye, p_end[u], 0.0) + state_side[u][:, :LANES]).astype(BF16),
                        state_side[u][:, LANES:])

    def finish(tok, w, prod):
        tr = slice(w * tw, (w + 1) * tw)
        state = {p: state_ref[p] for p in slabs}
        ys = {p: [] for p in slabs}
        for c in range(wave_chunks):
            for p in slabs:
                r_h, y_h, m_mat, g_mat = prod[(p, c)]
                sb = state[p].astype(BF16)
                y_st = _dg(r_h, sb) + y_h
                ys[p].append(y_st[:C] + y_st[C:])
                state[p] = _dg(m_mat, sb) + g_mat
            yield
        inv_n = 1.0 / HEAD_DIM
        for p in slabs:
            state_ref[p] = state[p]
            y = jnp.concatenate(ys[p], axis=0) if wave_chunks > 1 else ys[p][0]
            mean = seg_sum(y) * inv_n
            yc = y - mean
            var = seg_sum(yc * yc) * inv_n
            yn = yc * lax.rsqrt(var + LNX_EPS) * lw_ref[:, cols(p)] + lb_ref[:, cols(p)]
            y_ref[tr, cols(p)] = (yn + tok[p]["bonus"] * tok[p]["v"]) * g_ref[tr, cols(p)]
            yield

    def chain(gens):
        for g in gens:
            yield from g

    slabs = list(range(n_slab))
    tok = [dict() for _ in range(n_waves)]
    prod = [dict() for _ in range(n_waves)]

    def prepare_wave(w):
        for p in slabs:
            tok[w][p] = prepare(p, w)
            yield

    side = prepare_wave(0)
    for w in range(n_waves):
        for _ in side:
            pass
        side = chain(([finish(tok[w - 1], w - 1, prod[w - 1])] if w > 0 else [])
                     + ([prepare_wave(w + 1)] if w + 1 < n_waves else []))
        for _ in chunk_products(tok[w], [(p, c) for c in range(wave_chunks) for p in slabs], prod[w]):
            next(side, None)
    for _ in chain([side, finish(tok[n_waves - 1], n_waves - 1, prod[n_waves - 1])]):
        pass


def _rwkv_call(rkv, ld, ai, g, k_k, k_a, r_k, lnx_w, lnx_b, *, batch, seq, tb):
    M, W = ld.shape
    steps = seq // tb
    tok = lambda off: pl.BlockSpec((tb, W), lambda b, s: (b * steps + s, off))
    par = pl.BlockSpec((1, W), lambda b, s: (0, 0))
    return pl.pallas_call(
        functools.partial(_rwkv_kernel, chunk=RWKV_CHUNK),
        grid=(batch, steps),
        in_specs=[tok(0), tok(1), tok(2), tok(0), tok(0), tok(0), par, par, par, par, par],
        out_specs=tok(0),
        out_shape=jax.ShapeDtypeStruct((M, W), F32),
        scratch_shapes=[pltpu.VMEM((W // LANES, LANES, LANES), F32)],
        compiler_params=_cparams(("arbitrary", "arbitrary")),
        name="rwkv",
    )(rkv, rkv, rkv, ld, ai, g, k_k.reshape(1, W), k_a.reshape(1, W), r_k.reshape(1, W),
      lnx_w.reshape(1, W), lnx_b.reshape(1, W))


def _stick_kernel(qt_ref, k_ref, vt_ref, o_ref, acc_ref, run_ref, *, tile, group):
    first = pl.program_id(2) * group
    si = lax.broadcasted_iota(jnp.int32, (tile, 2 * tile), 0)
    ti = lax.broadcasted_iota(jnp.int32, (tile, 2 * tile), 1) % tile
    past = si < ti
    ui = lax.broadcasted_iota(jnp.int32, (tile, tile), 0)
    uj = lax.broadcasted_iota(jnp.int32, (tile, tile), 1)
    later_ones = (uj > ui).astype(BF16)
    feat0 = lax.broadcasted_iota(jnp.int32, (LANES, tile), 0) < HEAD_DIM
    zero = jnp.zeros((LANES, tile), BF16)
    q_pair = [jnp.concatenate([jnp.where(feat0, qt_ref[s], zero), jnp.where(feat0, zero, qt_ref[s])], axis=1)
              for s in range(group)]
    n = range(group)

    def visit(d, diagonal):
        kb, vt = [], []
        for s in n:
            jb = first + s - d
            blk = jnp.maximum(jb, 0)
            kb.append(k_ref[pl.ds(pl.multiple_of(blk * tile, tile), tile), :])
            vt.append(vt_ref[blk] if diagonal else jnp.where(jb >= 0, vt_ref[blk], zero))
        z = [_dg(kb[s], q_pair[s]) for s in n]
        ls = [_log_sigmoid(x) for x in z]
        lk = [ls[s] - z[s] for s in n]
        if diagonal:
            lk = [jnp.where(past, x, 0.0) for x in lk]
        later = [_dg(later_ones, lk[s].astype(BF16)) for s in n]
        worst = None
        for s in n:
            total = later[s][0:1, :] + lk[s][0:1, :]
            if diagonal:
                w = jnp.where(past, jnp.exp(ls[s] + later[s]), 0.0)
                run = total
            else:
                run = run_ref[s]
                w = jnp.exp(ls[s] + later[s] + run)
                run = run + total
            pv = _dg(vt[s], w.astype(BF16))
            pv = jnp.concatenate([pv[:HEAD_DIM, :tile], pv[HEAD_DIM:, tile:]], axis=0)
            acc_ref[s] = pv if diagonal else acc_ref[s] + pv
            run_ref[s] = run
            worst = run if worst is None else jnp.maximum(worst, run)
        return worst

    worst = visit(0, True)
    for d in range(1, SB_FIXED_VISITS):
        worst = visit(d, False)

    def cond(carry):
        d, worst = carry
        return (d < first + group) & (worst > -EXP_UNDERFLOW)

    def body(carry):
        d, _ = carry
        for j in range(SB_VISITS_PER_TEST):
            worst = visit(d + j, False)
        return d + SB_VISITS_PER_TEST, jnp.max(worst)

    lax.while_loop(cond, body, (jnp.int32(SB_FIXED_VISITS), jnp.max(worst)))
    for s in n:
        o_ref[s * tile:(s + 1) * tile, :] = acc_ref[s].T.astype(o_ref.dtype)


def _stick_call(ksb, qt, vt, *, batch, seq, group):
    M, W = ksb.shape
    n_slab = W // LANES
    tile = SB_TILE
    rows = tile * group
    steps = seq // rows
    blocks = seq // tile
    return pl.pallas_call(
        functools.partial(_stick_kernel, tile=tile, group=group),
        grid=(batch, n_slab, steps),
        in_specs=[pl.BlockSpec((group, LANES, tile), lambda b, p, i: (b * steps + i, p, 0)),
                  pl.BlockSpec((seq, LANES), lambda b, p, i: (b, p)),
                  pl.BlockSpec((blocks, LANES, tile), lambda b, p, i: (b, p, 0))],
        out_specs=pl.BlockSpec((rows, LANES), lambda b, p, i: (b * steps + i, p)),
        out_shape=jax.ShapeDtypeStruct((M, W), BF16),
        scratch_shapes=[pltpu.VMEM((group, LANES, tile), F32), pltpu.VMEM((group, 1, 2 * tile), F32)],
        compiler_params=_cparams(("arbitrary", "arbitrary", "arbitrary")),
        name="stick",
    )(qt, ksb, vt)


def _post_kernel(x_ref, ya_ref, yb_ref, gate_ref, mod_ref, wa_ref, wb_ref, wo_ref, g2_ref, gf_ref,
                 wg_ref, wu_ref, wd_ref, o_ref, *, final_norm):
    D = x_ref.shape[1]
    gt1, sh2, sc2, gt2 = (mod_ref[0, j:j + 1, :] for j in (2, 3, 4, 5))
    pa = _dg(ya_ref[...].astype(BF16), wa_ref[...])
    pb = _dg(yb_ref[...], wb_ref[...])
    merged = gate_ref[:, :D].astype(F32) * pa + gate_ref[:, D:].astype(F32) * pb
    x1 = x_ref[...] + gt1 * _dg(merged.astype(BF16), wo_ref[...])

    ms = jnp.mean(x1 * x1, axis=-1, keepdims=True)
    hb = ((x1 * lax.rsqrt(ms + NORM_EPS) * g2_ref[...]) * (1.0 + sc2) + sh2).astype(BF16)
    ug = _dg(hb, wg_ref[...])
    uu = _dg(hb, wu_ref[...])
    act = (ug * _sigmoid(ug) * uu).astype(BF16)
    x2 = x1 + gt2 * _dg(act, wd_ref[...])
    if final_norm:
        ms = jnp.mean(x2 * x2, axis=-1, keepdims=True)
        x2 = x2 * lax.rsqrt(ms + NORM_EPS) * gf_ref[...]
    o_ref[...] = x2


def _post_call(x2, ya, yb, gates, mod3, w_pa, w_pb, w_out, norm2_g, final_g, w_g, w_u, w_d,
               *, seq, tm, final_norm):
    M, D = x2.shape
    tiles_per_seq = seq // tm
    const = _const_spec
    rows = lambda n: pl.BlockSpec((tm, n), lambda i: (i, 0))
    return pl.pallas_call(
        functools.partial(_post_kernel, final_norm=final_norm),
        grid=(M // tm,),
        in_specs=[rows(D), rows(ya.shape[1]), rows(yb.shape[1]), rows(gates.shape[1]),
                  pl.BlockSpec((1, mod3.shape[1], D), lambda i: (i // tiles_per_seq, 0, 0)),
                  const(w_pa.shape), const(w_pb.shape), const(w_out.shape), const((1, D)), const((1, D)),
                  const(w_g.shape), const(w_u.shape), const(w_d.shape)],
        out_specs=rows(D),
        out_shape=jax.ShapeDtypeStruct((M, D), F32),
        compiler_params=_cparams(("arbitrary",)),
        name="post",
    )(x2, ya, yb, gates, mod3, w_pa, w_pb, w_out, norm2_g.reshape(1, D), final_g.reshape(1, D), w_g, w_u, w_d)


def _tiles(seq):
    pick = lambda want: max(t for t in (8, 16, 32, 64, 128, 256, 512, 1024) if t <= want and seq % t == 0)
    return dict(inproj=pick(512), rwkv=pick(512), stick=pick(SB_GROUP * SB_TILE), post=pick(512))


def kernel(x, c, w_ada, b_ada, norm1_g, w_in, mu_rkv, mu_wag, w0, w1, w2, a0, a1, a2, g1, g2, k_k, k_a, r_k,
           lnx_w, lnx_b, w_proj_a, w_proj_b, w_out, norm2_g, w_ffn_in, w_ffn_out, final_g):
    B, T, D = x.shape
    depth = w_ada.shape[0]
    W = w0.shape[1]
    S = w_proj_b.shape[1]
    F = w_ffn_out.shape[1]
    assert T % SB_TILE == 0 and W % LANES == 0 and S % LANES == 0
    tiles = _tiles(T)
    x2 = x.reshape(B * T, D)
    for l in range(depth):
        mod3 = _mod_call(c, w_ada[l], b_ada[l]).reshape(B, 6, D)
        wl = w_in[l].astype(BF16)
        w_q, w_k, w_v = (wl[:, 3 * W + j * S:3 * W + (j + 1) * S] for j in range(3))
        rkv, ld, ai, g, ksb, qt, vt, gates = _inproj_call(
            x2, mod3, norm1_g[l], wl[:, :3 * W], w_k, jnp.concatenate([w_q, w_v], axis=1).T,
            wl[:, 3 * W + 3 * S:],
            mu_rkv[l], mu_wag[l], w1[l].astype(BF16), w2[l].astype(BF16), w0[l],
            a1[l].astype(BF16), a2[l].astype(BF16), a0[l], g1[l].astype(BF16), g2[l].astype(BF16),
            seq=T, tm=tiles["inproj"])
        ya = _rwkv_call(rkv, ld, ai, g, k_k[l], k_a[l], r_k[l], lnx_w[l], lnx_b[l],
                        batch=B, seq=T, tb=tiles["rwkv"])
        yb = _stick_call(ksb, qt, vt, batch=B, seq=T, group=tiles["stick"] // SB_TILE)
        wf = w_ffn_in[l].astype(BF16)
        x2 = _post_call(x2, ya, yb, gates, mod3, w_proj_a[l].astype(BF16), w_proj_b[l].astype(BF16),
                        w_out[l].astype(BF16), norm2_g[l], final_g, wf[:, :F], wf[:, F:],
                        w_ffn_out[l].astype(BF16), seq=T, tm=tiles["post"], final_norm=(l == depth - 1))
    return x2.reshape(B, T, D)
```

```python
import functools

import jax
import jax.numpy as jnp
from jax import lax
from jax.experimental import pallas as pl
from jax.experimental.pallas import tpu as pltpu

F32 = jnp.float32
BF16 = jnp.bfloat16

HEAD_DIM = 64
LANES = 128
NORM_EPS = 1e-6
LNX_EPS = 64e-5
RWKV_CHUNK = 64
RWKV_WAVE_UNITS = 16
SB_TILE = 128
SB_GROUP = 8
SB_FIXED_VISITS = 4
SB_VISITS_PER_TEST = 2
EXP_UNDERFLOW = 104.0
VMEM_LIMIT = 56 * 1024 * 1024


def _cparams(sem, fuse_inputs=None):
    return pltpu.CompilerParams(dimension_semantics=sem, vmem_limit_bytes=VMEM_LIMIT,
                                allow_input_fusion=fuse_inputs)


def _const_spec(shape):
    return pl.BlockSpec(shape, lambda i: (0,) * len(shape), pipeline_mode=pl.Buffered(1))


_NN = (((1,), (0,)), ((), ()))
_NT = (((1,), (1,)), ((), ()))
_TN = (((0,), (0,)), ((), ()))


def _dg(a, b, dn=_NN):
    return lax.dot_general(a, b, dn, preferred_element_type=F32)


def _split(x, n):
    pieces, rem = [], x
    for i in range(n):
        p = rem.astype(BF16)
        pieces.append(p)
        if i + 1 < n:
            rem = rem - p.astype(F32)
    return pieces


def _mm(a, b, dn=_NN, na=1, nb=1):
    pa = _split(a, na) if a.dtype != BF16 else [a]
    pb = _split(b, nb) if b.dtype != BF16 else [b]
    order = max(len(pa), len(pb))
    acc = None
    for i, x in enumerate(pa):
        for j, y in enumerate(pb):
            if i + j < order:
                t = _dg(x, y, dn)
                acc = t if acc is None else acc + t
    return acc


def _sigmoid(x):
    return 1.0 / (1.0 + jnp.exp(-x))


def _log_sigmoid(x):
    return jnp.minimum(x, 0.0) - jnp.log(1.0 + jnp.exp(-jnp.abs(x)))


def _mod_kernel(c_ref, w_ref, b_ref, o_ref):
    c = c_ref[...]
    c_act = c * _sigmoid(c)
    o_ref[...] = _mm(c_act, w_ref[...], na=2, nb=2) + b_ref[...]


def _mod_call(c, w_ada, b_ada):
    B, D = c.shape
    n = w_ada.shape[1]
    return pl.pallas_call(
        _mod_kernel,
        grid=(n // D,),
        in_specs=[pl.BlockSpec((B, D), lambda j: (0, 0)),
                  pl.BlockSpec((D, D), lambda j: (0, j)),
                  pl.BlockSpec((1, D), lambda j: (0, j))],
        out_specs=pl.BlockSpec((B, D), lambda j: (0, j)),
        out_shape=jax.ShapeDtypeStruct((B, n), F32),
        compiler_params=_cparams(("arbitrary",)),
        name="mod",
    )(c, w_ada, b_ada.reshape(1, n))


def _shift_rows(cur, prev_row):
    rolled = pltpu.roll(cur, 1, 0)
    row = lax.broadcasted_iota(jnp.int32, cur.shape, 0)
    return jnp.where(row == 0, prev_row, rolled)


def _inproj_kernel(x_ref, mod_ref, g_ref, wrkv_ref, wk_ref, wqvt_ref, wgate_ref, murkv_ref, muwag_ref,
                   w1_ref, w2_ref, w0_ref, a1_ref, a2_ref, a0_ref, g1_ref, g2_ref,
                   rkv_ref, ld_ref, ai_ref, go_ref, ksb_ref, qt_ref, vt_ref, gate_ref, hlast_ref, plast_ref,
                   *, tiles_per_seq):
    tm = x_ref.shape[0]

    @pl.when((pl.program_id(0) % tiles_per_seq) == 0)
    def _():
        hlast_ref[...] = jnp.zeros_like(hlast_ref)
        plast_ref[...] = jnp.zeros_like(plast_ref)

    shift = mod_ref[0, 0:1, :]
    scale = mod_ref[0, 1:2, :]
    ms = jnp.mean(x_ref[...] * x_ref[...], axis=-1, keepdims=True)
    h = (x_ref[...] * lax.rsqrt(ms + NORM_EPS) * g_ref[...]) * (1.0 + scale) + shift
    hb = h.astype(BF16)

    gate_ref[...] = _sigmoid(_dg(hb, wgate_ref[...])).astype(BF16)

    xx = _shift_rows(h, hlast_ref[...]) - h
    hlast_ref[...] = h[tm - 1:tm, :]
    xw = (h + xx * muwag_ref[0:1, :]).astype(BF16)
    xa = (h + xx * muwag_ref[1:2, :]).astype(BF16)
    xg = (h + xx * muwag_ref[2:3, :]).astype(BF16)
    lw = _dg(xw, w1_ref[...])
    la = _dg(xa, a1_ref[...])
    lg = _dg(xg, g1_ref[...])

    p = _dg(hb, wrkv_ref[...])
    rkv_ref[...] = p + (_shift_rows(p, plast_ref[...]) - p) * murkv_ref[...]
    plast_ref[...] = p[tm - 1:tm, :]

    u = w0_ref[...] + _dg(jnp.tanh(lw).astype(BF16), w2_ref[...])
    w_log = _log_sigmoid(u) - 0.5
    ld_ref[...] = -jnp.exp(w_log)
    ai_ref[...] = _sigmoid(a0_ref[...] + _dg(la.astype(BF16), a2_ref[...]))
    go_ref[...] = _dg(_sigmoid(lg).astype(BF16), g2_ref[...])

    ksb_ref[...] = _dg(hb, wk_ref[...]).astype(BF16)
    qv = _dg(wqvt_ref[...], hb, _NT)
    n_feat = qt_ref.shape[1]
    for blk in range(x_ref.shape[0] // SB_TILE):
        cols = slice(blk * SB_TILE, (blk + 1) * SB_TILE)
        qt_ref[blk] = (qv[:n_feat, cols] * (HEAD_DIM ** -0.5)).astype(BF16)
        vt_ref[blk] = qv[n_feat:, cols].astype(BF16)


def _inproj_call(x2, mod3, norm_g, w_rkv, w_k, w_qvt, w_gate, mu_rkv, mu_wag, w1, w2, w0, a1, a2, a0, g1, g2,
                 *, seq, tm):
    M, D = x2.shape
    W3 = w_rkv.shape[1]
    W = W3 // 3
    S = w_k.shape[1]
    tiles_per_seq = seq // tm
    const = _const_spec
    rows = lambda n: pl.BlockSpec((tm, n), lambda i: (i, 0))
    feat_major = pl.BlockSpec((tm // SB_TILE, S, SB_TILE), lambda i: (i, 0, 0))
    return pl.pallas_call(
        functools.partial(_inproj_kernel, tiles_per_seq=tiles_per_seq),
        grid=(M // tm,),
        in_specs=[rows(D),
                  pl.BlockSpec((1, mod3.shape[1], D), lambda i: (i // tiles_per_seq, 0, 0)),
                  const((1, D)), const(w_rkv.shape), const(w_k.shape), const(w_qvt.shape), const(w_gate.shape),
                  const((1, W3)), const(mu_wag.shape),
                  const(w1.shape), const(w2.shape), const((1, W)),
                  const(a1.shape), const(a2.shape), const((1, W)),
                  const(g1.shape), const(g2.shape)],
        out_specs=[rows(W3), rows(W), rows(W), rows(W), rows(S), feat_major, feat_major, rows(w_gate.shape[1])],
        out_shape=[jax.ShapeDtypeStruct((M, W3), F32),
                   jax.ShapeDtypeStruct((M, W), F32),
                   jax.ShapeDtypeStruct((M, W), F32),
                   jax.ShapeDtypeStruct((M, W), F32),
                   jax.ShapeDtypeStruct((M, S), BF16),
                   jax.ShapeDtypeStruct((M // SB_TILE, S, SB_TILE), BF16),
                   jax.ShapeDtypeStruct((M // SB_TILE, S, SB_TILE), BF16),
                   jax.ShapeDtypeStruct((M, w_gate.shape[1]), BF16)],
        scratch_shapes=[pltpu.VMEM((1, D), F32), pltpu.VMEM((1, W3), F32)],
        compiler_params=_cparams(("arbitrary",), [i in (3, 4, 6, 9, 10, 12, 13, 15, 16) for i in range(17)]),
        name="inproj",
    )(x2, mod3, norm_g.reshape(1, D), w_rkv, w_k, w_qvt, w_gate, mu_rkv.reshape(1, W3), mu_wag,
      w1, w2, w0.reshape(1, W), a1, a2, a0.reshape(1, W), g1, g2)


def _rwkv_kernel(r_ref, k_ref, v_ref, ld_ref, ai_ref, g_ref, kk_ref, ka_ref, rk_ref, lw_ref, lb_ref,
                 y_ref, state_ref, *, chunk):
    C = chunk
    tb, W = r_ref.shape
    n_slab = W // LANES
    n_chunks = tb // C
    S2 = 2 * C

    @pl.when(pl.program_id(1) == 0)
    def _():
        state_ref[...] = jnp.zeros_like(state_ref)

    lane = lax.broadcasted_iota(jnp.int32, (1, LANES), 1)
    head0 = lane < HEAD_DIM
    li = lax.broadcasted_iota(jnp.int32, (LANES, LANES), 0)
    lj = lax.broadcasted_iota(jnp.int32, (LANES, LANES), 1)
    seg_ones = ((li // HEAD_DIM) == (lj // HEAD_DIM)).astype(BF16)
    li2 = lax.broadcasted_iota(jnp.int32, (2 * LANES, 2 * LANES), 0)
    lj2 = lax.broadcasted_iota(jnp.int32, (2 * LANES, 2 * LANES), 1)
    seg_ones2 = ((li2 // HEAD_DIM) == (lj2 // HEAD_DIM)).astype(BF16)
    n_waves = 2 if n_chunks % 2 == 0 and (n_chunks // 2) * n_slab >= RWKV_WAVE_UNITS else 1
    wave_chunks = n_chunks // n_waves
    tw = wave_chunks * C
    ti = lax.broadcasted_iota(jnp.int32, (tw, tw), 0)
    tj = lax.broadcasted_iota(jnp.int32, (tw, tw), 1)
    cum_ones = ((ti // C == tj // C) & (tj <= ti)).astype(BF16)
    si = lax.broadcasted_iota(jnp.int32, (S2, S2), 0)
    sj = lax.broadcasted_iota(jnp.int32, (S2, S2), 1)
    same = (si // C) == (sj // C)
    strict = same & (sj < si)
    incl = same & (sj <= si)
    eye = si == sj
    level_masks = []
    bsz = 1
    while bsz < C:
        level_masks.append(((si // (2 * bsz)) == (sj // (2 * bsz)))
                           & (((si // bsz) % 2) == 1) & (((sj // bsz) % 2) == 0))
        bsz *= 2

    def seg_sum(x):
        return _dg(x.astype(BF16), seg_ones)

    def stack(x):
        return jnp.concatenate([jnp.where(head0, x, 0.0), jnp.where(head0, 0.0, x)], axis=0)

    zeros = jnp.zeros((S2, LANES), BF16)
    rows = lambda c: slice(c * C, (c + 1) * C)
    cols = lambda p: slice(p * LANES, (p + 1) * LANES)

    def odd_rows(x, b):
        return jnp.concatenate([x[i * b:(i + 1) * b] for i in range(1, S2 // b, 2)], axis=0)

    def add_to_odd_rows(x, upd, b):
        return jnp.concatenate([x[i * b:(i + 1) * b] + upd[(i // 2) * b:(i // 2 + 1) * b] if i % 2 else
                                x[i * b:(i + 1) * b] for i in range(S2 // b)], axis=0)

    def prepare(p, w):
        tr = slice(w * tw, (w + 1) * tw)
        ld = ld_ref[tr, cols(p)]
        ai = ai_ref[tr, cols(p)]
        k = k_ref[tr, cols(p)]
        r = r_ref[tr, cols(p)]
        kk = k * kk_ref[:, cols(p)]
        k2 = k * (1.0 + (ai - 1.0) * ka_ref[:, cols(p)])
        sums = _dg(jnp.concatenate([kk * kk, r * k2 * rk_ref[:, cols(p)]], axis=1).astype(BF16), seg_ones2)
        kk = kk / jnp.maximum(jnp.sqrt(sums[:, :LANES]), 1e-12)
        cum = _mm(cum_ones, ld, nb=2)
        e_in = jnp.exp(cum)
        e_neg = jnp.exp(-cum)
        return dict(v=v_ref[tr, cols(p)], bonus=sums[:, LANES:], e_in=e_in, a_t=-kk * jnp.exp(cum - ld),
                    r_t=r * e_in, b_t=kk * ai * e_neg, k_t=k2 * e_neg)

    def chunk_products(tok, units, out):
        n = range(len(units))
        p_end = [tok[p]["e_in"][(c + 1) * C - 1:(c + 1) * C, :] for p, c in units]
        a_s = [stack(tok[p]["a_t"][rows(c)]).astype(BF16) for p, c in units]
        r_s = [stack(tok[p]["r_t"][rows(c)]) for p, c in units]
        v_s = [stack(tok[p]["v"][rows(c)]).astype(BF16) for p, c in units]
        bk = [jnp.concatenate([stack(tok[p]["b_t"][rows(c)]), stack(tok[p]["k_t"][rows(c)])], axis=0)
              for p, c in units]
        sc = [_dg(jnp.concatenate([a_s[u], r_s[u].astype(BF16)], axis=0), bk[u].astype(BF16), _NT) for u in n]
        yield
        a_ab = [jnp.where(strict, sc[u][:S2, :S2], 0.0) for u in n]
        a_ak = [jnp.where(strict, sc[u][:S2, S2:], 0.0).astype(BF16) for u in n]
        a_rb = [jnp.where(incl, sc[u][S2:, :S2], 0.0).astype(BF16) for u in n]
        a_rk = [jnp.where(incl, sc[u][S2:, S2:], 0.0).astype(BF16) for u in n]
        pinv = [jnp.where(eye, 1.0, jnp.where(level_masks[0], a_ab[u], 0.0)) for u in n]
        for lvl, mask in enumerate(level_masks[1:], start=1):
            b = 1 << lvl
            pb = [pinv[u].astype(BF16) for u in n]
            lower = [jnp.where(mask, a_ab[u], 0.0).astype(BF16) for u in n]
            if b % 16 == 0:
                xl = [_dg(odd_rows(pb[u], b), lower[u]) for u in n]
                yield
                pinv = [add_to_odd_rows(pinv[u], _dg(xl[u].astype(BF16), pb[u]), b) for u in n]
            else:
                xl = [_dg(pb[u], lower[u]) for u in n]
                yield
                pinv = [pinv[u] + _dg(xl[u].astype(BF16), pb[u]) for u in n]
            yield
        av = [_dg(a_ak[u], v_s[u]) for u in n]
        yield
        ta = [_dg(pinv[u].astype(BF16), jnp.concatenate([a_s[u], av[u].astype(BF16)], axis=1)).astype(BF16)
              for u in n]
        yield
        rhs = [jnp.concatenate([ta[u], jnp.concatenate([zeros, v_s[u]], axis=1)], axis=0) for u in n]
        out_side = [_dg(jnp.concatenate([a_rb[u], a_rk[u]], axis=1), rhs[u]) for u in n]
        state_side = [_dg((bk[u] * p_end[u]).astype(BF16), rhs[u], _TN) for u in n]
        for u, key in enumerate(units):
            out[key] = ((r_s[u] + out_side[u][:, :LANES]).astype(BF16),
                        out_side[u][:, LANES:],
                        (jnp.where(eye, p_end[u], 0.0) + state_side[u][:, :LANES]).astype(BF16),
                        state_side[u][:, LANES:])

    def finish(tok, w, prod):
        tr = slice(w * tw, (w + 1) * tw)
        state = {p: state_ref[p] for p in slabs}
        ys = {p: [] for p in slabs}
        for c in range(wave_chunks):
            for p in slabs:
                r_h, y_h, m_mat, g_mat = prod[(p, c)]
                sb = state[p].astype(BF16)
                y_st = _dg(r_h, sb) + y_h
                ys[p].append(y_st[:C] + y_st[C:])
                state[p] = _dg(m_mat, sb) + g_mat
            yield
        inv_n = 1.0 / HEAD_DIM
        for p in slabs:
            state_ref[p] = state[p]
            y = jnp.concatenate(ys[p], axis=0) if wave_chunks > 1 else ys[p][0]
            mean = seg_sum(y) * inv_n
            yc = y - mean
            var = seg_sum(yc * yc) * inv_n
            yn = yc * lax.rsqrt(var + LNX_EPS) * lw_ref[:, cols(p)] + lb_ref[:, cols(p)]
            y_ref[tr, cols(p)] = (yn + tok[p]["bonus"] * tok[p]["v"]) * g_ref[tr, cols(p)]
            yield

    def chain(gens):
        for g in gens:
            yield from g

    slabs = list(range(n_slab))
    tok = [dict() for _ in range(n_waves)]
    prod = [dict() for _ in range(n_waves)]

    def prepare_wave(w):
        for p in slabs:
            tok[w][p] = prepare(p, w)
            yield

    side = prepare_wave(0)
    for w in range(n_waves):
        for _ in side:
            pass
        side = chain(([finish(tok[w - 1], w - 1, prod[w - 1])] if w > 0 else [])
                     + ([prepare_wave(w + 1)] if w + 1 < n_waves else []))
        for _ in chunk_products(tok[w], [(p, c) for c in range(wave_chunks) for p in slabs], prod[w]):
            next(side, None)
    for _ in chain([side, finish(tok[n_waves - 1], n_waves - 1, prod[n_waves - 1])]):
        pass


def _rwkv_call(rkv, ld, ai, g, k_k, k_a, r_k, lnx_w, lnx_b, *, batch, seq, tb):
    M, W = ld.shape
    steps = seq // tb
    tok = lambda off: pl.BlockSpec((tb, W), lambda b, s: (b * steps + s, off))
    par = pl.BlockSpec((1, W), lambda b, s: (0, 0))
    return pl.pallas_call(
        functools.partial(_rwkv_kernel, chunk=RWKV_CHUNK),
        grid=(batch, steps),
        in_specs=[tok(0), tok(1), tok(2), tok(0), tok(0), tok(0), par, par, par, par, par],
        out_specs=tok(0),
        out_shape=jax.ShapeDtypeStruct((M, W), F32),
        scratch_shapes=[pltpu.VMEM((W // LANES, LANES, LANES), F32)],
        compiler_params=_cparams(("arbitrary", "arbitrary")),
        name="rwkv",
    )(rkv, rkv, rkv, ld, ai, g, k_k.reshape(1, W), k_a.reshape(1, W), r_k.reshape(1, W),
      lnx_w.reshape(1, W), lnx_b.reshape(1, W))


def _stick_kernel(qt_ref, k_ref, vt_ref, o_ref, acc_ref, run_ref, *, tile, group):
    first = pl.program_id(2) * group
    si = lax.broadcasted_iota(jnp.int32, (tile, 2 * tile), 0)
    ti = lax.broadcasted_iota(jnp.int32, (tile, 2 * tile), 1) % tile
    past = si < ti
    ui = lax.broadcasted_iota(jnp.int32, (tile, tile), 0)
    uj = lax.broadcasted_iota(jnp.int32, (tile, tile), 1)
    later_ones = (uj > ui).astype(BF16)
    feat0 = lax.broadcasted_iota(jnp.int32, (LANES, tile), 0) < HEAD_DIM
    zero = jnp.zeros((LANES, tile), BF16)
    q_pair = [jnp.concatenate([jnp.where(feat0, qt_ref[s], zero), jnp.where(feat0, zero, qt_ref[s])], axis=1)
              for s in range(group)]
    n = range(group)

    def visit(d, diagonal):
        kb, vt = [], []
        for s in n:
            jb = first + s - d
            blk = jnp.maximum(jb, 0)
            kb.append(k_ref[pl.ds(pl.multiple_of(blk * tile, tile), tile), :])
            vt.append(vt_ref[blk] if diagonal else jnp.where(jb >= 0, vt_ref[blk], zero))
        z = [_dg(kb[s], q_pair[s]) for s in n]
        ls = [_log_sigmoid(x) for x in z]
        lk = [ls[s] - z[s] for s in n]
        if diagonal:
            lk = [jnp.where(past, x, 0.0) for x in lk]
        later = [_dg(later_ones, lk[s].astype(BF16)) for s in n]
        worst = None
        for s in n:
            total = later[s][0:1, :] + lk[s][0:1, :]
            if diagonal:
                w = jnp.where(past, jnp.exp(ls[s] + later[s]), 0.0)
                run = total
            else:
                run = run_ref[s]
                w = jnp.exp(ls[s] + later[s] + run)
                run = run + total
            pv = _dg(vt[s], w.astype(BF16))
            pv = jnp.concatenate([pv[:HEAD_DIM, :tile], pv[HEAD_DIM:, tile:]], axis=0)
            acc_ref[s] = pv if diagonal else acc_ref[s] + pv
            run_ref[s] = run
            worst = run if worst is None else jnp.maximum(worst, run)
        return worst

    worst = visit(0, True)
    for d in range(1, SB_FIXED_VISITS):
        worst = visit(d, False)

    def cond(carry):
        d, worst = carry
        return (d < first + group) & (worst > -EXP_UNDERFLOW)

    def body(carry):
        d, _ = carry
        for j in range(SB_VISITS_PER_TEST):
            worst = visit(d + j, False)
        return d + SB_VISITS_PER_TEST, jnp.max(worst)

    lax.while_loop(cond, body, (jnp.int32(SB_FIXED_VISITS), jnp.max(worst)))
    for s in n:
        o_ref[s * tile:(s + 1) * tile, :] = acc_ref[s].T.astype(o_ref.dtype)


def _stick_call(ksb, qt, vt, *, batch, seq, group):
    M, W = ksb.shape
    n_slab = W // LANES
    tile = SB_TILE
    rows = tile * group
    steps = seq // rows
    blocks = seq // tile
    return pl.pallas_call(
        functools.partial(_stick_kernel, tile=tile, group=group),
        grid=(batch, n_slab, steps),
        in_specs=[pl.BlockSpec((group, LANES, tile), lambda b, p, i: (b * steps + i, p, 0)),
                  pl.BlockSpec((seq, LANES), lambda b, p, i: (b, p)),
                  pl.BlockSpec((blocks, LANES, tile), lambda b, p, i: (b, p, 0))],
        out_specs=pl.BlockSpec((rows, LANES), lambda b, p, i: (b * steps + i, p)),
        out_shape=jax.ShapeDtypeStruct((M, W), BF16),
        scratch_shapes=[pltpu.VMEM((group, LANES, tile), F32), pltpu.VMEM((group, 1, 2 * tile), F32)],
        compiler_params=_cparams(("arbitrary", "arbitrary", "arbitrary")),
        name="stick",
    )(qt, ksb, vt)


def _post_kernel(x_ref, ya_ref, yb_ref, gate_ref, mod_ref, wa_ref, wb_ref, wo_ref, g2_ref, gf_ref,
                 wg_ref, wu_ref, wd_ref, o_ref, *, final_norm):
    D = x_ref.shape[1]
    gt1, sh2, sc2, gt2 = (mod_ref[0, j:j + 1, :] for j in (2, 3, 4, 5))
    pa = _dg(ya_ref[...].astype(BF16), wa_ref[...])
    pb = _dg(yb_ref[...], wb_ref[...])
    merged = gate_ref[:, :D].astype(F32) * pa + gate_ref[:, D:].astype(F32) * pb
    x1 = x_ref[...] + gt1 * _dg(merged.astype(BF16), wo_ref[...])

    ms = jnp.mean(x1 * x1, axis=-1, keepdims=True)
    hb = ((x1 * lax.rsqrt(ms + NORM_EPS) * g2_ref[...]) * (1.0 + sc2) + sh2).astype(BF16)
    ug = _dg(hb, wg_ref[...])
    uu = _dg(hb, wu_ref[...])
    act = (ug * _sigmoid(ug) * uu).astype(BF16)
    x2 = x1 + gt2 * _dg(act, wd_ref[...])
    if final_norm:
        ms = jnp.mean(x2 * x2, axis=-1, keepdims=True)
        x2 = x2 * lax.rsqrt(ms + NORM_EPS) * gf_ref[...]
    o_ref[...] = x2


def _post_call(x2, ya, yb, gates, mod3, w_pa, w_pb, w_out, norm2_g, final_g, w_g, w_u, w_d,
               *, seq, tm, final_norm):
    M, D = x2.shape
    tiles_per_seq = seq // tm
    const = _const_spec
    rows = lambda n: pl.BlockSpec((tm, n), lambda i: (i, 0))
    return pl.pallas_call(
        functools.partial(_post_kernel, final_norm=final_norm),
        grid=(M // tm,),
        in_specs=[rows(D), rows(ya.shape[1]), rows(yb.shape[1]), rows(gates.shape[1]),
                  pl.BlockSpec((1, mod3.shape[1], D), lambda i: (i // tiles_per_seq, 0, 0)),
                  const(w_pa.shape), const(w_pb.shape), const(w_out.shape), const((1, D)), const((1, D)),
                  const(w_g.shape), const(w_u.shape), const(w_d.shape)],
        out_specs=rows(D),
        out_shape=jax.ShapeDtypeStruct((M, D), F32),
        compiler_params=_cparams(("arbitrary",), [False] * 5 + [True] * 3 + [False] * 2 + [True] * 3),
        name="post",
    )(x2, ya, yb, gates, mod3, w_pa, w_pb, w_out, norm2_g.reshape(1, D), final_g.reshape(1, D), w_g, w_u, w_d)


def _tiles(seq):
    pick = lambda want: max(t for t in (8, 16, 32, 64, 128, 256, 512, 1024) if t <= want and seq % t == 0)
    return dict(inproj=pick(512), rwkv=pick(512), stick=pick(SB_GROUP * SB_TILE), post=pick(512))


def kernel(x, c, w_ada, b_ada, norm1_g, w_in, mu_rkv, mu_wag, w0, w1, w2, a0, a1, a2, g1, g2, k_k, k_a, r_k,
           lnx_w, lnx_b, w_proj_a, w_proj_b, w_out, norm2_g, w_ffn_in, w_ffn_out, final_g):
    B, T, D = x.shape
    depth = w_ada.shape[0]
    W = w0.shape[1]
    S = w_proj_b.shape[1]
    F = w_ffn_out.shape[1]
    assert T % SB_TILE == 0 and W % LANES == 0 and S % LANES == 0
    tiles = _tiles(T)
    x2 = x.reshape(B * T, D)
    for l in range(depth):
        mod3 = _mod_call(c, w_ada[l], b_ada[l]).reshape(B, 6, D)
        wl = w_in[l].astype(BF16)
        w_q, w_k, w_v = (wl[:, 3 * W + j * S:3 * W + (j + 1) * S] for j in range(3))
        rkv, ld, ai, g, ksb, qt, vt, gates = _inproj_call(
            x2, mod3, norm1_g[l], wl[:, :3 * W], w_k, jnp.concatenate([w_q, w_v], axis=1).T,
            wl[:, 3 * W + 3 * S:],
            mu_rkv[l], mu_wag[l], w1[l].astype(BF16), w2[l].astype(BF16), w0[l],
            a1[l].astype(BF16), a2[l].astype(BF16), a0[l], g1[l].astype(BF16), g2[l].astype(BF16),
            seq=T, tm=tiles["inproj"])
        ya = _rwkv_call(rkv, ld, ai, g, k_k[l], k_a[l], r_k[l], lnx_w[l], lnx_b[l],
                        batch=B, seq=T, tb=tiles["rwkv"])
        yb = _stick_call(ksb, qt, vt, batch=B, seq=T, group=tiles["stick"] // SB_TILE)
        wf = w_ffn_in[l].astype(BF16)
        x2 = _post_call(x2, ya, yb, gates, mod3, w_proj_a[l].astype(BF16), w_proj_b[l].astype(BF16),
                        w_out[l].astype(BF16), norm2_g[l], final_g, wf[:, :F], wf[:, F:],
                        w_ffn_out[l].astype(BF16), seq=T, tm=tiles["post"], final_norm=(l == depth - 1))
    return x2.reshape(B, T, D)
```
